```python
import jax, jax.numpy as jnp
from jax import lax
import numpy as np

D_MODEL = 1024
BATCH = 2
SEQ = 8192
DEPTH = 2

N_MIXERS = 2
HEAD_DIM = 64
N_HEADS = D_MODEL // HEAD_DIM
FOX_HEADS = N_HEADS
DSA_Q_HEADS = N_HEADS
DSA_KV_HEADS = 4
GROUP = DSA_Q_HEADS // DSA_KV_HEADS
IDX_HEADS = 8
IDX_DIM = 64
TOPK_MAX = 256
Q_BLOCK = 128
D_FF = 2816
CONV_WIDTH = 3
EPS = 1e-6
FORGET_BIAS = 3.0
N_FOX_LAYERS = (DEPTH + 1) // 2
N_DSA_LAYERS = DEPTH // 2

FOX_IN = 3 * FOX_HEADS * HEAD_DIM + FOX_HEADS
DSA_QW = DSA_Q_HEADS * HEAD_DIM
DSA_KVW = DSA_KV_HEADS * HEAD_DIM
DSA_SPLITS = [DSA_QW, DSA_QW + DSA_KVW, DSA_QW + 2 * DSA_KVW,
              DSA_QW + 2 * DSA_KVW + IDX_HEADS * IDX_DIM,
              DSA_QW + 2 * DSA_KVW + IDX_HEADS * IDX_DIM + IDX_DIM]
DSA_IN = DSA_SPLITS[-1] + IDX_HEADS

kernel_name = 'hybrid_fox_dsa_convffn'


def rms_norm(x, g):
    xf = x.astype(jnp.float32)
    y = xf * lax.rsqrt(jnp.mean(xf * xf, axis=-1, keepdims=True) + EPS)
    return (y * g.astype(jnp.float32)).astype(x.dtype)


def modulate(x, g, shift, scale):
    return rms_norm(x, g) * (1 + scale[:, None, :]) + shift[:, None, :]


def alibi_slopes(n):
    return jnp.exp2(-8.0 * jnp.arange(1, n + 1, dtype=jnp.float32) / n)


def to_blocks(a):
    b, l = a.shape[:2]
    return jnp.moveaxis(a.reshape(b, l // Q_BLOCK, Q_BLOCK, *a.shape[2:]), 1, 0)


def from_blocks(a):
    nb, b, q = a.shape[:3]
    return jnp.moveaxis(a, 0, 1).reshape(b, nb * q, *a.shape[3:])


def forgetting_attention(h, w_in, b_f, g_q, g_k, w_out):
    b, l, _ = h.shape
    dq = FOX_HEADS * HEAD_DIM
    q, k, v, f_logit = jnp.split(h @ w_in, [dq, 2 * dq, 3 * dq], axis=-1)
    q = rms_norm(q.reshape(b, l, FOX_HEADS, HEAD_DIM), g_q)
    k = rms_norm(k.reshape(b, l, FOX_HEADS, HEAD_DIM), g_k).astype(jnp.float32)
    v = v.reshape(b, l, FOX_HEADS, HEAD_DIM)
    log_f = jax.nn.log_sigmoid((f_logit + b_f).astype(jnp.float32))
    cum = jnp.cumsum(log_f, axis=1)
    cum_k = jnp.transpose(cum, (0, 2, 1))
    pos = jnp.arange(l)
    scale = HEAD_DIM ** -0.5

    def block(args):
        qb, cqb, pb = args
        s = jnp.einsum('bqhd,bkhd->bhqk', qb.astype(jnp.float32), k) * scale
        s = s + jnp.transpose(cqb, (0, 2, 1))[..., None] - cum_k[:, :, None, :]
        s = jnp.where(pos[None, :] <= pb[:, None], s, -jnp.inf)
        p = jax.nn.softmax(s, axis=-1)
        return jnp.einsum('bhqk,bkhd->bqhd', p.astype(v.dtype), v)

    o = lax.map(block, (to_blocks(q), to_blocks(cum), pos.reshape(-1, Q_BLOCK)))
    o = from_blocks(o).reshape(b, l, dq)
    return o @ w_out


def sparse_indexed_attention(h, w_in, g_q, g_k, g_kidx, w_out):
    b, l, _ = h.shape
    top_k = min(TOPK_MAX, l // 4)
    q, k, v, q_idx, k_idx, w_idx = jnp.split(h @ w_in, DSA_SPLITS, axis=-1)
    q = rms_norm(q.reshape(b, l, DSA_KV_HEADS, GROUP, HEAD_DIM), g_q)
    k = rms_norm(k.reshape(b, l, DSA_KV_HEADS, HEAD_DIM), g_k)
    v = v.reshape(b, l, DSA_KV_HEADS, HEAD_DIM)
    q_idx = q_idx.reshape(b, l, IDX_HEADS, IDX_DIM)
    k_idx = rms_norm(k_idx, g_kidx).astype(jnp.float32)
    w_idx = w_idx.astype(jnp.float32) * (IDX_HEADS ** -0.5 * IDX_DIM ** -0.5)
    slopes = alibi_slopes(DSA_Q_HEADS).reshape(DSA_KV_HEADS, GROUP)
    pos = jnp.arange(l)
    scale = HEAD_DIM ** -0.5
    gather = jax.vmap(lambda a, i: a[i])

    def block(args):
        qb, qib, wib, pb = args
        rel = jax.nn.relu(jnp.einsum('bqhd,bkd->bqhk', qib.astype(jnp.float32), k_idx))
        score = jnp.einsum('bqhk,bqh->bqk', rel, wib)
        score = jnp.where(pos[None, None, :] <= pb[None, :, None], score, -jnp.inf)
        _, idx = lax.top_k(score, top_k)
        valid = idx <= pb[None, :, None]
        k_sel = gather(k, idx).astype(jnp.float32)
        v_sel = gather(v, idx)
        s = jnp.einsum('bqgrd,bqkgd->bqgrk', qb.astype(jnp.float32), k_sel) * scale
        dist = (pb[None, :, None] - idx).astype(jnp.float32)
        s = s - slopes[None, None, :, :, None] * dist[:, :, None, None, :]
        s = jnp.where(valid[:, :, None, None, :], s, -jnp.inf)
        p = jax.nn.softmax(s, axis=-1)
        return jnp.einsum('bqgrk,bqkgd->bqgrd', p.astype(v_sel.dtype), v_sel)

    o = lax.map(block, (to_blocks(q), to_blocks(q_idx), to_blocks(w_idx),
                        pos.reshape(-1, Q_BLOCK)))
    o = from_blocks(o).reshape(b, l, DSA_QW)
    return o @ w_out


def conv_gated_mlp(h, w_up, conv_w, conv_b, w_down):
    a, g = jnp.split(h @ w_up, 2, axis=-1)
    a = lax.conv_general_dilated(a, conv_w[:, None, :], window_strides=(1,),
                                 padding=[(CONV_WIDTH - 1, 0)],
                                 dimension_numbers=('NWC', 'WIO', 'NWC'),
                                 feature_group_count=D_FF) + conv_b
    return (jax.nn.silu(a) * g) @ w_down


def setup_inputs(seed: int = 0) -> dict:
    key = jax.random.key(seed)
    ks = jax.random.split(key, 24)
    f32 = jnp.float32

    def dense(k, shape, fan_in, mult=1.0):
        return jax.random.normal(k, shape, f32) * (mult * fan_in ** -0.5)

    def gain(k, shape):
        return 1.0 + 0.02 * jax.random.normal(k, shape, f32)

    return {
        'x': jax.random.normal(ks[0], (BATCH, SEQ, D_MODEL), f32),
        'c': jax.random.normal(ks[1], (BATCH, D_MODEL), f32),
        'w_ada': dense(ks[2], (DEPTH, D_MODEL, 6 * D_MODEL), D_MODEL, 0.5),
        'b_ada': 0.01 * jax.random.normal(ks[3], (DEPTH, 6 * D_MODEL), f32),
        'g_norm_mix': gain(ks[4], (DEPTH, D_MODEL)),
        'g_norm_ffn': gain(ks[5], (DEPTH, D_MODEL)),
        'fox_w_in': dense(ks[6], (N_FOX_LAYERS, D_MODEL, FOX_IN), D_MODEL),
        'fox_b_f': FORGET_BIAS + 0.1 * jax.random.normal(ks[7], (N_FOX_LAYERS, FOX_HEADS), f32),
        'fox_g_q': gain(ks[8], (N_FOX_LAYERS, HEAD_DIM)),
        'fox_g_k': gain(ks[9], (N_FOX_LAYERS, HEAD_DIM)),
        'fox_w_out': dense(ks[10], (N_FOX_LAYERS, FOX_HEADS * HEAD_DIM, D_MODEL), FOX_HEADS * HEAD_DIM),
        'dsa_w_in': dense(ks[11], (N_DSA_LAYERS, D_MODEL, DSA_IN), D_MODEL),
        'dsa_g_q': gain(ks[12], (N_DSA_LAYERS, HEAD_DIM)),
        'dsa_g_k': gain(ks[13], (N_DSA_LAYERS, HEAD_DIM)),
        'dsa_g_kidx': gain(ks[14], (N_DSA_LAYERS, IDX_DIM)),
        'dsa_w_out': dense(ks[15], (N_DSA_LAYERS, DSA_QW, D_MODEL), DSA_QW),
        'ffn_w_up': dense(ks[16], (DEPTH, D_MODEL, 2 * D_FF), D_MODEL),
        'ffn_conv_w': dense(ks[17], (DEPTH, CONV_WIDTH, D_FF), CONV_WIDTH),
        'ffn_conv_b': 0.01 * jax.random.normal(ks[18], (DEPTH, D_FF), f32),
        'ffn_w_down': dense(ks[19], (DEPTH, D_FF, D_MODEL), D_FF),
    }


def reference(x, c, w_ada, b_ada, g_norm_mix, g_norm_ffn, fox_w_in, fox_b_f, fox_g_q,
              fox_g_k, fox_w_out, dsa_w_in, dsa_g_q, dsa_g_k, dsa_g_kidx, dsa_w_out,
              ffn_w_up, ffn_conv_w, ffn_conv_b, ffn_w_down):
    c_act = jax.nn.silu(c)
    for i in range(DEPTH):
        mod = c_act @ w_ada[i] + b_ada[i]
        sh1, sc1, gt1, sh2, sc2, gt2 = jnp.split(mod, 6, axis=-1)
        h = modulate(x, g_norm_mix[i], sh1, sc1)
        j = i // N_MIXERS
        if i % N_MIXERS == 0:
            y = forgetting_attention(h, fox_w_in[j], fox_b_f[j], fox_g_q[j], fox_g_k[j], fox_w_out[j])
        else:
            y = sparse_indexed_attention(h, dsa_w_in[j], dsa_g_q[j], dsa_g_k[j], dsa_g_kidx[j], dsa_w_out[j])
        x = x + gt1[:, None, :] * y
        h = modulate(x, g_norm_ffn[i], sh2, sc2)
        x = x + gt2[:, None, :] * conv_gated_mlp(h, ffn_w_up[i], ffn_conv_w[i], ffn_conv_b[i], ffn_w_down[i])
    return x
```

```python
import functools
import math

import jax
import jax.numpy as jnp
from jax import lax
from jax.experimental import pallas as pl
from jax.experimental.pallas import tpu as pltpu

F32 = jnp.float32
BF16 = jnp.bfloat16
I32 = jnp.int32

HEAD_DIM = 64
DSA_KV_HEADS = 4
IDX_HEADS = 8
IDX_DIM = 64
TOPK_MAX = 256
EPS = 1e-6
LOG2E = 1.4426950408889634
LANES = 128
VMEM_LIMIT = 56 * 1024 * 1024
NEG_BIG = -1e30
KEY_NEG_INF = -0x7F800000
KEY_POS_INF = 0x7F800000
SEARCH_STEPS = 32


def _params(*sem):
    return pltpu.CompilerParams(dimension_semantics=sem, vmem_limit_bytes=VMEM_LIMIT)


def _resident(shape, index_map):
    return pl.BlockSpec(shape, index_map, pipeline_mode=pl.Buffered(1))


def _round_up(n, m):
    return (n + m - 1) // m * m


def _adaln_kernel(c_ref, w_ref, b_ref, o_ref):
    c = c_ref[...]
    ca = c * jax.nn.sigmoid(c)
    o_ref[0] = jnp.dot(ca, w_ref[0], preferred_element_type=F32) + b_ref[0]


def _adaln(c, w_ada, b_ada):
    depth, d, n = w_ada.shape
    b = c.shape[0]
    rows = _round_up(b, 8)
    c_pad = jnp.pad(c, ((0, rows - b), (0, 0)))
    tn = 1536
    out = pl.pallas_call(
        _adaln_kernel,
        grid=(depth, n // tn),
        in_specs=[
            pl.BlockSpec((rows, d), lambda i, j: (0, 0)),
            pl.BlockSpec((1, d, tn), lambda i, j: (i, 0, j)),
            pl.BlockSpec((1, 1, tn), lambda i, j: (i, 0, j)),
        ],
        out_specs=pl.BlockSpec((1, rows, tn), lambda i, j: (i, 0, j)),
        out_shape=jax.ShapeDtypeStruct((depth, rows, n), F32),
        compiler_params=_params("arbitrary", "arbitrary"),
        name="adaln",
    )(c_pad, w_ada, b_ada.reshape(depth, 1, n))
    return out[:, :b]


def _modulated(x, g, sh, sc):
    ms = jnp.mean(x * x, axis=-1, keepdims=True)
    return x * lax.rsqrt(ms + EPS) * (g * (1.0 + sc)) + sh


def _inproj_kernel(x_ref, g_ref, sh_ref, sc_ref, w_ref, o_ref):
    h = _modulated(x_ref[0], g_ref[...], sh_ref[0], sc_ref[0])
    o_ref[0] = jnp.dot(h.astype(BF16), w_ref[...], preferred_element_type=F32)


def _inproj(x, g, sh, sc, w_bf16, tm=512):
    b, l, d = x.shape
    n = w_bf16.shape[1]
    tm = min(tm, l)
    return pl.pallas_call(
        _inproj_kernel,
        grid=(b, l // tm),
        in_specs=[
            pl.BlockSpec((1, tm, d), lambda i, j: (i, j, 0)),
            pl.BlockSpec((1, d), lambda i, j: (0, 0)),
            pl.BlockSpec((1, 1, d), lambda i, j: (i, 0, 0)),
            pl.BlockSpec((1, 1, d), lambda i, j: (i, 0, 0)),
            _resident((d, n), lambda i, j: (0, 0)),
        ],
        out_specs=pl.BlockSpec((1, tm, n), lambda i, j: (i, j, 0)),
        out_shape=jax.ShapeDtypeStruct((b, l, n), F32),
        compiler_params=_params("arbitrary", "arbitrary"),
        name="inproj",
    )(x, g.reshape(1, d), sh.reshape(b, 1, d), sc.reshape(b, 1, d), w_bf16)


def _knorm_kernel(x_ref, g_ref, o_ref):
    x = x_ref[...]
    ms = jnp.mean(x * x, axis=1, keepdims=True)
    o_ref[...] = (x * lax.rsqrt(ms + EPS) * g_ref[...]).astype(o_ref.dtype)


def _knorm(x, g, mb=16):
    m, dh, tk = x.shape
    mb = math.gcd(mb, m)
    return pl.pallas_call(
        _knorm_kernel,
        grid=(m // mb,),
        in_specs=[
            pl.BlockSpec((mb, dh, tk), lambda i: (i, 0, 0)),
            pl.BlockSpec((1, dh, 1), lambda i: (0, 0, 0)),
        ],
        out_specs=pl.BlockSpec((mb, dh, tk), lambda i: (i, 0, 0)),
        out_shape=jax.ShapeDtypeStruct((m, dh, tk), BF16),
        compiler_params=_params("arbitrary"),
        name="knorm",
    )(x, g.reshape(1, dh, 1))


def _split3(x):
    hi = x.astype(BF16)
    r = x - hi.astype(F32)
    mid = r.astype(BF16)
    lo = (r - mid.astype(F32)).astype(BF16)
    return hi, mid, lo


def _cumgate_kernel(f_ref, b_ref, o_ref, *, cw):
    x = f_ref[0] + b_ref[...]
    lf = jnp.minimum(x, 0.0) - jnp.log1p(jnp.exp(-jnp.abs(x)))
    h, l = lf.shape
    row = lax.broadcasted_iota(I32, (cw, cw), 0)
    col = lax.broadcasted_iota(I32, (cw, cw), 1)
    tri = jnp.where(row <= col, 1.0, 0.0).astype(BF16)
    carry = jnp.zeros((h, 1), F32)
    for c in range(l // cw):
        hi, mid, lo = _split3(lf[:, c * cw:(c + 1) * cw])
        cs = (jnp.dot(hi, tri, preferred_element_type=F32)
              + jnp.dot(mid, tri, preferred_element_type=F32)
              + jnp.dot(lo, tri, preferred_element_type=F32)) + carry
        o_ref[0, :, c * cw:(c + 1) * cw] = cs * LOG2E
        carry = cs[:, cw - 1:cw]


def _cumgate(f_t, b_f):
    b, h, l = f_t.shape
    cw = min(256, l)
    return pl.pallas_call(
        functools.partial(_cumgate_kernel, cw=cw),
        grid=(b,),
        in_specs=[
            pl.BlockSpec((1, h, l), lambda i: (i, 0, 0)),
            pl.BlockSpec((h, 1), lambda i: (0, 0)),
        ],
        out_specs=pl.BlockSpec((1, h, l), lambda i: (i, 0, 0)),
        out_shape=jax.ShapeDtypeStruct((b, h, l), F32),
        compiler_params=_params("arbitrary"),
        name="cumgate",
    )(f_t, b_f.reshape(h, 1))


def _head_rms(q, g):
    ms = jnp.mean(q * q, axis=-1, keepdims=True)
    return q * lax.rsqrt(ms + EPS) * g


def _fox_kernel(q_ref, kt_ref, v_ref, c_ref, gq_ref, o_ref, va_ref, m_ref, acc_ref, *, t, scale):
    qi = pl.program_id(2)
    nk = va_ref.shape[0]

    @pl.when(qi == 0)
    def _():
        lane = lax.broadcasted_iota(I32, (nk, t, LANES - HEAD_DIM), 2)
        va_ref[:, :, :HEAD_DIM] = v_ref[0, 0]
        va_ref[:, :, HEAD_DIM:] = jnp.where(lane == 0, 1.0, 0.0).astype(BF16)

    qb = (_head_rms(q_ref[0, 0], gq_ref[...]) * (scale * LOG2E)).astype(BF16)
    m_ref[...] = jnp.full(m_ref.shape, -jnp.inf, F32)
    acc_ref[...] = jnp.zeros(acc_ref.shape, F32)

    def chunk(j, diagonal):
        s = jnp.dot(qb, kt_ref[0, 0, j], preferred_element_type=F32) - c_ref[0, 0, j]
        if diagonal:
            row = lax.broadcasted_iota(I32, (t, t), 0)
            col = lax.broadcasted_iota(I32, (t, t), 1)
            s = jnp.where(col <= row, s, -jnp.inf)
        m_prev = m_ref[...]
        m_new = jnp.maximum(m_prev, jnp.max(s, axis=-1, keepdims=True))
        p = jnp.exp2(s - m_new)
        alpha = jnp.exp2(m_prev - m_new)
        acc_ref[...] = alpha * acc_ref[...] + jnp.dot(p.astype(BF16), va_ref[j], preferred_element_type=F32)
        m_ref[...] = m_new

    def body(j, carry):
        chunk(j, False)
        return carry

    lax.fori_loop(0, qi, body, 0)
    chunk(qi, True)
    acc = acc_ref[...]
    o_ref[0, 0] = (acc[:, :HEAD_DIM] / acc[:, HEAD_DIM:HEAD_DIM + 1]).astype(o_ref.dtype)


def _fox_attention(q, kt, v, cum, g_q, t):
    b, h, l, dh = q.shape
    nk = l // t
    return pl.pallas_call(
        functools.partial(_fox_kernel, t=t, scale=dh ** -0.5),
        grid=(b, h, nk),
        in_specs=[
            pl.BlockSpec((1, 1, t, dh), lambda i, j, k: (i, j, k, 0)),
            pl.BlockSpec((1, 1, nk, dh, t), lambda i, j, k: (i, j, 0, 0, 0)),
            pl.BlockSpec((1, 1, nk, t, dh), lambda i, j, k: (i, j, 0, 0, 0)),
            pl.BlockSpec((1, 1, nk, 1, t), lambda i, j, k: (i, j, 0, 0, 0)),
            pl.BlockSpec((1, dh), lambda i, j, k: (0, 0)),
        ],
        out_specs=pl.BlockSpec((1, 1, t, dh), lambda i, j, k: (i, j, k, 0)),
        out_shape=jax.ShapeDtypeStruct((b, h, l, dh), BF16),
        scratch_shapes=[
            pltpu.VMEM((nk, t, LANES), BF16),
            pltpu.VMEM((t, 1), F32),
            pltpu.VMEM((t, LANES), F32),
        ],
        compiler_params=_params("arbitrary", "arbitrary", "arbitrary"),
        name="fox_attn",
    )(q, kt, v, cum, g_q.reshape(1, dh))


def _ffn_kernel(x_ref, o_ref, wo_ref, gt1_ref, g_ref, sh_ref, sc_ref, gt2_ref,
                wa_ref, wg_ref, cw_ref, cb_ref, wd_ref, out_ref, abuf_ref, *, tm):
    i = pl.program_id(1)
    y = jnp.dot(o_ref[0], wo_ref[...], preferred_element_type=F32)
    x1 = x_ref[0] + gt1_ref[0] * y
    h = _modulated(x1, g_ref[...], sh_ref[0], sc_ref[0]).astype(BF16)
    a = jnp.dot(h, wa_ref[...], preferred_element_type=F32)
    gate = jnp.dot(h, wg_ref[...], preferred_element_type=F32)

    @pl.when(i == 0)
    def _():
        abuf_ref[0:8, :] = jnp.zeros((8, abuf_ref.shape[1]), F32)

    @pl.when(i > 0)
    def _():
        abuf_ref[0:8, :] = abuf_ref[tm:tm + 8, :]

    abuf_ref[8:8 + tm, :] = a
    a1 = abuf_ref[7:7 + tm, :]
    a2 = abuf_ref[6:6 + tm, :]
    cw = cw_ref[...]
    ac = cw[0:1] * a2 + cw[1:2] * a1 + cw[2:3] * a + cb_ref[...]
    act = (ac * jax.nn.sigmoid(ac) * gate).astype(BF16)
    out_ref[0] = x1 + gt2_ref[0] * jnp.dot(act, wd_ref[...], preferred_element_type=F32)


def _post_attention_ffn(x, o, wo, gt1, g, sh, sc, gt2, wa, wg, cw, cb, wd, tm=256):
    b, l, d = x.shape
    f = wa.shape[1]
    tm = min(tm, l)
    row = lambda i, j: (i, j, 0)
    per_batch = lambda i, j: (i, 0, 0)
    const = lambda i, j: (0, 0)
    return pl.pallas_call(
        functools.partial(_ffn_kernel, tm=tm),
        grid=(b, l // tm),
        in_specs=[
            pl.BlockSpec((1, tm, d), row),
            pl.BlockSpec((1, tm, d), row),
            _resident((d, d), const),
            pl.BlockSpec((1, 1, d), per_batch),
            pl.BlockSpec((1, d), const),
            pl.BlockSpec((1, 1, d), per_batch),
            pl.BlockSpec((1, 1, d), per_batch),
            pl.BlockSpec((1, 1, d), per_batch),
            _resident((d, f), const),
            _resident((d, f), const),
            pl.BlockSpec((3, f), const),
            pl.BlockSpec((1, f), const),
            _resident((f, d), const),
        ],
        out_specs=pl.BlockSpec((1, tm, d), row),
        out_shape=jax.ShapeDtypeStruct((b, l, d), F32),
        scratch_shapes=[pltpu.VMEM((tm + 8, f), F32)],
        compiler_params=_params("arbitrary", "arbitrary"),
        name="post_attn_ffn",
    )(x, o, wo, gt1.reshape(b, 1, d), g.reshape(1, d), sh.reshape(b, 1, d), sc.reshape(b, 1, d),
      gt2.reshape(b, 1, d), wa, wg, cw, cb.reshape(1, f), wd)


def _key_to_float(key):
    bits = jnp.where(key < 0, jnp.int32(-2 ** 31) - key, key)
    return lax.bitcast_convert_type(bits, F32)


def _dsa_kernel(q_ref, qi_ref, wi_ref, kt_ref, va_ref, kit_ref, gq_ref, o_ref,
                s_ref, m_ref, acc_ref, *, t, top_k, scale, slopes):
    qt = pl.program_id(1)
    n_groups = kt_ref.shape[1]
    group = q_ref.shape[1] // n_groups
    nch = qt + 1
    row = lax.broadcasted_iota(I32, (t, t), 0)
    col = lax.broadcasted_iota(I32, (t, t), 1)

    def positions_valid(j):
        return (j * t + col) <= (qt * t + row)

    w = wi_ref[0] * (IDX_HEADS ** -0.5 * IDX_DIM ** -0.5)

    def score_chunk(j, carry):
        kc = kit_ref[0, j]
        sc = jnp.zeros((t, t), F32)
        for hh in range(IDX_HEADS):
            rel = jnp.maximum(jnp.dot(qi_ref[0, hh], kc, preferred_element_type=F32), 0.0)
            sc = sc + rel * w[:, hh:hh + 1]
        s_ref[j] = jnp.where(positions_valid(j), sc, -jnp.inf)
        return carry

    lax.fori_loop(0, nch, score_chunk, 0)

    def count(pred):
        def body(j, c):
            return c + jnp.sum(jnp.where(pred(s_ref[j], j), 1.0, 0.0), axis=-1, keepdims=True)
        return lax.fori_loop(0, nch, body, jnp.zeros((t, 1), F32))

    def search(_):
        def step(_, lohi):
            lo, hi = lohi
            mid = (lo >> 1) + (hi >> 1) + (lo & hi & 1)
            c = _key_to_float(mid)
            ge = count(lambda s, j: s >= c) >= top_k
            return jnp.where(ge, mid, lo), jnp.where(ge, hi, mid)
        lo0 = jnp.full((t, 1), KEY_NEG_INF, I32)
        hi0 = jnp.full((t, 1), KEY_POS_INF + 1, I32)
        lo, _ = lax.fori_loop(0, SEARCH_STEPS, step, (lo0, hi0))
        return _key_to_float(lo)

    tau = lax.cond(nch * t > top_k, search, lambda _: jnp.full((t, 1), -jnp.inf, F32), 0)

    n_gt = count(lambda s, j: s > tau)
    n_eq = count(lambda s, j: s == tau)
    need = top_k - n_gt
    tie = (n_eq > need) & (tau > -jnp.inf)
    any_tie = jnp.max(jnp.where(tie, 1.0, 0.0)) > 0.0

    def tie_cut(_):
        def step(_, lohi):
            lo, hi = lohi
            mid = (lo + hi) >> 1
            ge = count(lambda s, j: (s == tau) & ((j * t + col) <= mid)) >= need
            return jnp.where(ge, lo, mid), jnp.where(ge, mid, hi)
        lo0 = jnp.full((t, 1), -1, I32)
        hi0 = jnp.full((t, 1), nch * t - 1, I32)
        steps = max(1, int(math.ceil(math.log2(s_ref.shape[0] * t + 1))))
        _, hi = lax.fori_loop(0, steps, step, (lo0, hi0))
        return jnp.where(tie, hi, jnp.int32(2 ** 30))

    jstar = lax.cond(any_tie, tie_cut, lambda _: jnp.full((t, 1), 2 ** 30, I32), 0)

    def mask_chunk(j, carry):
        s = s_ref[j]
        sel = (s > tau) | ((s == tau) & ((j * t + col) <= jstar))
        sel = sel & positions_valid(j)
        s_ref[j] = jnp.where(sel, 0.0, NEG_BIG)
        return carry

    lax.fori_loop(0, nch, mask_chunk, 0)

    gq = gq_ref[...]
    pos = (lax.broadcasted_iota(I32, (1, 1, t), 2) - qt * t).astype(F32)
    for g in range(n_groups):
        qs = []
        for r in range(group):
            qn = _head_rms(q_ref[0, g * group + r], gq) * (scale * LOG2E)
            qs.append(qn.astype(BF16))
        qb = jnp.concatenate(qs, axis=0)
        slope = jnp.concatenate(
            [jnp.full((1, 1, 1), slopes[g * group + r] * LOG2E, F32) for r in range(group)], axis=0)
        m_ref[...] = jnp.full(m_ref.shape, NEG_BIG, F32)
        acc_ref[...] = jnp.zeros(acc_ref.shape, F32)

        def att_chunk(j, carry, g=g, qb=qb, slope=slope):
            s = jnp.dot(qb, kt_ref[0, g, j], preferred_element_type=F32)
            bias = s_ref[j][None] + slope * (pos + (j * t).astype(F32))
            s = (s.reshape(group, t, t) + bias).reshape(group * t, t)
            m_prev = m_ref[...]
            m_new = jnp.maximum(m_prev, jnp.max(s, axis=-1, keepdims=True))
            p = jnp.exp2(s - m_new)
            alpha = jnp.exp2(m_prev - m_new)
            acc_ref[...] = alpha * acc_ref[...] + jnp.dot(p.astype(BF16), va_ref[0, g, j],
                                                          preferred_element_type=F32)
            m_ref[...] = m_new
            return carry

        lax.fori_loop(0, nch, att_chunk, 0)
        acc = acc_ref[...]
        out = acc[:, :HEAD_DIM] / acc[:, HEAD_DIM:HEAD_DIM + 1]
        for r in range(group):
            o_ref[0, g * group + r] = out[r * t:(r + 1) * t].astype(o_ref.dtype)


def _dsa_attention(q, q_idx, w_idx, kt, va, kit, g_q, t, top_k):
    b, hq, l, dh = q.shape
    n_groups = kt.shape[1]
    nk = l // t
    slopes = tuple(2.0 ** (-8.0 * (i + 1) / hq) for i in range(hq))
    rows = (hq // n_groups) * t
    return pl.pallas_call(
        functools.partial(_dsa_kernel, t=t, top_k=top_k, scale=dh ** -0.5, slopes=slopes),
        grid=(b, nk),
        in_specs=[
            pl.BlockSpec((1, hq, t, dh), lambda i, j: (i, 0, j, 0)),
            pl.BlockSpec((1, IDX_HEADS, t, IDX_DIM), lambda i, j: (i, 0, j, 0)),
            pl.BlockSpec((1, t, IDX_HEADS), lambda i, j: (i, j, 0)),
            _resident((1, n_groups, nk, dh, t), lambda i, j: (i, 0, 0, 0, 0)),
            _resident((1, n_groups, nk, t, LANES), lambda i, j: (i, 0, 0, 0, 0)),
            _resident((1, nk, IDX_DIM, t), lambda i, j: (i, 0, 0, 0)),
            pl.BlockSpec((1, dh), lambda i, j: (0, 0)),
        ],
        out_specs=pl.BlockSpec((1, hq, t, dh), lambda i, j: (i, 0, j, 0)),
        out_shape=jax.ShapeDtypeStruct((b, hq, l, dh), BF16),
        scratch_shapes=[
            pltpu.VMEM((nk, t, t), F32),
            pltpu.VMEM((rows, 1), F32),
            pltpu.VMEM((rows, LANES), F32),
        ],
        compiler_params=_params("arbitrary", "arbitrary"),
        name="dsa_attn",
    )(q, q_idx, w_idx, kt, va, kit, g_q.reshape(1, dh))


def _heads_major(a, n_heads):
    b, l, _ = a.shape
    return a.reshape(b, l, n_heads, -1).transpose(0, 2, 1, 3)


def _keys_transposed(a, n_heads, t):
    b, l, _ = a.shape
    return a.reshape(b, l // t, t, n_heads, -1).transpose(0, 3, 1, 4, 2)


def _values_chunked(a, n_heads, t):
    b, l, _ = a.shape
    return a.reshape(b, l // t, t, n_heads, -1).transpose(0, 3, 1, 2, 4)


def _merge_heads(o):
    b, h, l, dh = o.shape
    return o.transpose(0, 2, 1, 3).reshape(b, l, h * dh)


def _pad_cols(w, mult=LANES):
    n = w.shape[1]
    return jnp.pad(w, ((0, 0), (0, _round_up(n, mult) - n)))


def kernel(x, c, w_ada, b_ada, g_norm_mix, g_norm_ffn, fox_w_in, fox_b_f, fox_g_q, fox_g_k, fox_w_out,
           dsa_w_in, dsa_g_q, dsa_g_k, dsa_g_kidx, dsa_w_out, ffn_w_up, ffn_conv_w, ffn_conv_b, ffn_w_down):
    b, l, d = x.shape
    dh = HEAD_DIM
    n_heads = d // dh
    d_ff = ffn_w_down.shape[1]
    top_k = min(TOPK_MAX, l // 4)
    t = min(256, l)

    mod = _adaln(c, w_ada, b_ada)

    def ffn(i, xx, o, w_out):
        sh2, sc2, gt2 = mod[i, :, 3 * d:4 * d], mod[i, :, 4 * d:5 * d], mod[i, :, 5 * d:6 * d]
        gt1 = mod[i, :, 2 * d:3 * d]
        wup = ffn_w_up[i].astype(BF16)
        return _post_attention_ffn(xx, o, w_out.astype(BF16), gt1, g_norm_ffn[i], sh2, sc2, gt2,
                                   wup[:, :d_ff], wup[:, d_ff:], ffn_conv_w[i], ffn_conv_b[i],
                                   ffn_w_down[i].astype(BF16))

    proj = _inproj(x, g_norm_mix[0], mod[0, :, 0:d], mod[0, :, d:2 * d], _pad_cols(fox_w_in[0]).astype(BF16))
    q = _heads_major(proj[..., 0:d], n_heads)
    kt_raw = _keys_transposed(proj[..., d:2 * d], n_heads, t)
    v = _values_chunked(proj[..., 2 * d:3 * d], n_heads, t).astype(BF16)
    f_t = proj[..., 3 * d:3 * d + n_heads].transpose(0, 2, 1)
    kt = _knorm(kt_raw.reshape(-1, dh, t), fox_g_k[0]).reshape(kt_raw.shape)
    cum = _cumgate(f_t, fox_b_f[0]).reshape(b, n_heads, l // t, 1, t)
    o = _fox_attention(q, kt, v, cum, fox_g_q[0], t)
    x = ffn(0, x, _merge_heads(o), fox_w_out[0])

    g_kv = DSA_KV_HEADS
    kvw = g_kv * dh
    proj = _inproj(x, g_norm_mix[1], mod[1, :, 0:d], mod[1, :, d:2 * d], _pad_cols(dsa_w_in[0]).astype(BF16))
    o0, o1, o2, o3, o4 = d, d + kvw, d + 2 * kvw, d + 2 * kvw + IDX_HEADS * IDX_DIM, d + 2 * kvw + IDX_HEADS * IDX_DIM + IDX_DIM
    q = _heads_major(proj[..., 0:o0], n_heads)
    kt_raw = _keys_transposed(proj[..., o0:o1], g_kv, t)
    v = _values_chunked(proj[..., o1:o2], g_kv, t).astype(BF16)
    ones = jnp.zeros(v.shape[:-1] + (LANES - dh,), BF16).at[..., 0].set(1)
    va = jnp.concatenate([v, ones], axis=-1)
    q_idx = _heads_major(proj[..., o2:o3], IDX_HEADS).astype(BF16)
    kit_raw = _keys_transposed(proj[..., o3:o4], 1, t)
    w_idx = proj[..., o4:o4 + IDX_HEADS]
    kt = _knorm(kt_raw.reshape(-1, dh, t), dsa_g_k[0]).reshape(kt_raw.shape)
    kit = _knorm(kit_raw.reshape(-1, IDX_DIM, t), dsa_g_kidx[0]).reshape(b, l // t, IDX_DIM, t)
    o = _dsa_attention(q, q_idx, w_idx, kt, va, kit, dsa_g_q[0], t, top_k)
    x = ffn(1, x, _merge_heads(o), dsa_w_out[0])
    return x
```

```python
import functools
import math

import jax
import jax.numpy as jnp
from jax import lax
from jax.experimental import pallas as pl
from jax.experimental.pallas import tpu as pltpu

F32 = jnp.float32
BF16 = jnp.bfloat16
I32 = jnp.int32

HEAD_DIM = 64
DSA_KV_HEADS = 4
IDX_HEADS = 8
IDX_DIM = 64
TOPK_MAX = 256
EPS = 1e-6
LOG2E = 1.4426950408889634
LANES = 128
VMEM_LIMIT = 56 * 1024 * 1024
NEG_BIG = -1e30
F32_LOWEST = -3.4028234663852886e38
KEY_NEG_INF = -0x7F800000
KEY_POS_INF = 0x7F800000
Q_TILE = 256
K_TILE = 512


def _params(*sem):
    return pltpu.CompilerParams(dimension_semantics=sem, vmem_limit_bytes=VMEM_LIMIT)


def _resident(shape, index_map):
    return pl.BlockSpec(shape, index_map, pipeline_mode=pl.Buffered(1))


def _round_up(n, m):
    return (n + m - 1) // m * m


def _adaln_kernel(c_ref, w_ref, b_ref, o_ref):
    c = c_ref[...]
    ca = c * jax.nn.sigmoid(c)
    o_ref[0] = jnp.dot(ca, w_ref[0], preferred_element_type=F32) + b_ref[0]


def _adaln(c, w_ada, b_ada):
    depth, d, n = w_ada.shape
    b = c.shape[0]
    rows = _round_up(b, 8)
    c_pad = jnp.pad(c, ((0, rows - b), (0, 0)))
    tn = 1536
    out = pl.pallas_call(
        _adaln_kernel,
        grid=(depth, n // tn),
        in_specs=[
            pl.BlockSpec((rows, d), lambda i, j: (0, 0)),
            pl.BlockSpec((1, d, tn), lambda i, j: (i, 0, j)),
            pl.BlockSpec((1, 1, tn), lambda i, j: (i, 0, j)),
        ],
        out_specs=pl.BlockSpec((1, rows, tn), lambda i, j: (i, 0, j)),
        out_shape=jax.ShapeDtypeStruct((depth, rows, n), F32),
        compiler_params=_params("arbitrary", "arbitrary"),
        name="adaln",
    )(c_pad, w_ada, b_ada.reshape(depth, 1, n))
    return out[:, :b]


def _modulated(x, g, sh, sc):
    ms = jnp.mean(x * x, axis=-1, keepdims=True)
    return x * lax.rsqrt(ms + EPS) * (g * (1.0 + sc)) + sh


def _inproj_kernel(x_ref, g_ref, sh_ref, sc_ref, w_ref, o_ref):
    h = _modulated(x_ref[0], g_ref[...], sh_ref[0], sc_ref[0])
    o_ref[0] = jnp.dot(h.astype(BF16), w_ref[...], preferred_element_type=F32)


def _inproj(x, g, sh, sc, w_bf16, tm=512):
    b, l, d = x.shape
    n = w_bf16.shape[1]
    tm = min(tm, l)
    return pl.pallas_call(
        _inproj_kernel,
        grid=(b, l // tm),
        in_specs=[
            pl.BlockSpec((1, tm, d), lambda i, j: (i, j, 0)),
            pl.BlockSpec((1, d), lambda i, j: (0, 0)),
            pl.BlockSpec((1, 1, d), lambda i, j: (i, 0, 0)),
            pl.BlockSpec((1, 1, d), lambda i, j: (i, 0, 0)),
            _resident((d, n), lambda i, j: (0, 0)),
        ],
        out_specs=pl.BlockSpec((1, tm, n), lambda i, j: (i, j, 0)),
        out_shape=jax.ShapeDtypeStruct((b, l, n), F32),
        compiler_params=_params("arbitrary", "arbitrary"),
        name="inproj",
    )(x, g.reshape(1, d), sh.reshape(b, 1, d), sc.reshape(b, 1, d), w_bf16)


def _knorm_kernel(x_ref, g_ref, o_ref):
    x = x_ref[...]
    ms = jnp.mean(x * x, axis=1, keepdims=True)
    o_ref[...] = (x * lax.rsqrt(ms + EPS) * g_ref[...]).astype(o_ref.dtype)


def _knorm(x, g, mb=8):
    m, dh, tk = x.shape
    mb = math.gcd(mb, m)
    return pl.pallas_call(
        _knorm_kernel,
        grid=(m // mb,),
        in_specs=[
            pl.BlockSpec((mb, dh, tk), lambda i: (i, 0, 0)),
            pl.BlockSpec((1, dh, 1), lambda i: (0, 0, 0)),
        ],
        out_specs=pl.BlockSpec((mb, dh, tk), lambda i: (i, 0, 0)),
        out_shape=jax.ShapeDtypeStruct((m, dh, tk), BF16),
        compiler_params=_params("arbitrary"),
        name="knorm",
    )(x, g.reshape(1, dh, 1))


def _split3(x):
    hi = x.astype(BF16)
    r = x - hi.astype(F32)
    mid = r.astype(BF16)
    lo = (r - mid.astype(F32)).astype(BF16)
    return hi, mid, lo


def _cumgate_kernel(f_ref, b_ref, o_ref, *, cw):
    x = f_ref[0] + b_ref[...]
    lf = jnp.minimum(x, 0.0) - jnp.log1p(jnp.exp(-jnp.abs(x)))
    h, l = lf.shape
    row = lax.broadcasted_iota(I32, (cw, cw), 0)
    col = lax.broadcasted_iota(I32, (cw, cw), 1)
    tri = jnp.where(row <= col, 1.0, 0.0).astype(BF16)
    carry = jnp.zeros((h, 1), F32)
    for c in range(l // cw):
        hi, mid, lo = _split3(lf[:, c * cw:(c + 1) * cw])
        cs = (jnp.dot(hi, tri, preferred_element_type=F32)
              + jnp.dot(mid, tri, preferred_element_type=F32)
              + jnp.dot(lo, tri, preferred_element_type=F32)) + carry
        o_ref[0, :, c * cw:(c + 1) * cw] = cs * LOG2E
        carry = cs[:, cw - 1:cw]


def _cumgate(f_t, b_f):
    b, h, l = f_t.shape
    cw = min(256, l)
    return pl.pallas_call(
        functools.partial(_cumgate_kernel, cw=cw),
        grid=(b,),
        in_specs=[
            pl.BlockSpec((1, h, l), lambda i: (i, 0, 0)),
            pl.BlockSpec((h, 1), lambda i: (0, 0)),
        ],
        out_specs=pl.BlockSpec((1, h, l), lambda i: (i, 0, 0)),
        out_shape=jax.ShapeDtypeStruct((b, h, l), F32),
        compiler_params=_params("arbitrary"),
        name="cumgate",
    )(f_t, b_f.reshape(h, 1))


def _head_rms(q, g):
    ms = jnp.mean(q * q, axis=-1, keepdims=True)
    return q * lax.rsqrt(ms + EPS) * g


def _online_softmax_step(s, m_ref, acc_ref, va, idx):
    m_prev = m_ref[idx]
    m_new = jnp.maximum(m_prev, jnp.max(s, axis=-1, keepdims=True))
    p = jnp.exp2(s - jnp.tile(m_new, (1, s.shape[-1] // LANES)))
    alpha = jnp.exp2(m_prev - m_new)
    acc_ref[idx] = alpha * acc_ref[idx] + jnp.dot(p.astype(BF16), va, preferred_element_type=F32)
    m_ref[idx] = m_new


def _fox_kernel(q_ref, kt_ref, v_ref, c_ref, gq_ref, o_ref, va_ref, m_ref, acc_ref, *, tq, tk, hb, scale):
    qi = pl.program_id(2)
    nk = va_ref.shape[1]

    @pl.when(qi == 0)
    def _():
        lane = lax.broadcasted_iota(I32, (nk, tk, LANES - HEAD_DIM), 2)
        ones_col = jnp.where(lane == 0, 1.0, 0.0).astype(BF16)
        for h in range(hb):
            va_ref[h, :, :, :HEAD_DIM] = v_ref[0, h]
            va_ref[h, :, :, HEAD_DIM:] = ones_col

    gq = gq_ref[...]
    qbs = [(_head_rms(q_ref[0, h], gq) * (scale * LOG2E)).astype(BF16) for h in range(hb)]
    m_ref[...] = jnp.full(m_ref.shape, -jnp.inf, F32)
    acc_ref[...] = jnp.zeros(acc_ref.shape, F32)
    nch = ((qi + 1) * tq + tk - 1) // tk

    def chunk(j, last):
        for h in range(hb):
            s = jnp.dot(qbs[h], kt_ref[0, h, j], preferred_element_type=F32) - c_ref[0, h, j]
            if last:
                row = lax.broadcasted_iota(I32, (tq, tk), 0) + qi * tq
                col = lax.broadcasted_iota(I32, (tq, tk), 1) + j * tk
                s = jnp.where(col <= row, s, -jnp.inf)
            _online_softmax_step(s, m_ref, acc_ref, va_ref[h, j], h)

    def body(j, carry):
        chunk(j, False)
        return carry

    lax.fori_loop(0, nch - 1, body, 0)
    chunk(nch - 1, True)
    for h in range(hb):
        acc = acc_ref[h]
        o_ref[0, h] = (acc[:, :HEAD_DIM] / acc[:, HEAD_DIM:HEAD_DIM + 1]).astype(o_ref.dtype)


def _fox_attention(q, kt, v, cum, g_q, tq, tk, hb=4):
    b, h, l, dh = q.shape
    nk = l // tk
    hb = math.gcd(hb, h)
    return pl.pallas_call(
        functools.partial(_fox_kernel, tq=tq, tk=tk, hb=hb, scale=dh ** -0.5),
        grid=(b, h // hb, l // tq),
        in_specs=[
            pl.BlockSpec((1, hb, tq, dh), lambda i, j, k: (i, j, k, 0)),
            _resident((1, hb, nk, dh, tk), lambda i, j, k: (i, j, 0, 0, 0)),
            _resident((1, hb, nk, tk, dh), lambda i, j, k: (i, j, 0, 0, 0)),
            _resident((1, hb, nk, 1, tk), lambda i, j, k: (i, j, 0, 0, 0)),
            pl.BlockSpec((1, dh), lambda i, j, k: (0, 0)),
        ],
        out_specs=pl.BlockSpec((1, hb, tq, dh), lambda i, j, k: (i, j, k, 0)),
        out_shape=jax.ShapeDtypeStruct((b, h, l, dh), BF16),
        scratch_shapes=[
            pltpu.VMEM((hb, nk, tk, LANES), BF16),
            pltpu.VMEM((hb, tq, LANES), F32),
            pltpu.VMEM((hb, tq, LANES), F32),
        ],
        compiler_params=_params("arbitrary", "arbitrary", "arbitrary"),
        name="fox_attn",
    )(q, kt, v, cum, g_q.reshape(1, dh))


def _ffn_kernel(x_ref, o_ref, wo_ref, gt1_ref, g_ref, sh_ref, sc_ref, gt2_ref,
                wa_ref, wg_ref, cw_ref, cb_ref, wd_ref, out_ref, abuf_ref, *, tm):
    i = pl.program_id(1)
    y = jnp.dot(o_ref[0], wo_ref[...], preferred_element_type=F32)
    x1 = x_ref[0] + gt1_ref[0] * y
    h = _modulated(x1, g_ref[...], sh_ref[0], sc_ref[0]).astype(BF16)
    a = jnp.dot(h, wa_ref[...], preferred_element_type=F32)
    gate = jnp.dot(h, wg_ref[...], preferred_element_type=F32)

    @pl.when(i == 0)
    def _():
        abuf_ref[0:8, :] = jnp.zeros((8, abuf_ref.shape[1]), F32)

    @pl.when(i > 0)
    def _():
        abuf_ref[0:8, :] = abuf_ref[tm:tm + 8, :]

    abuf_ref[8:8 + tm, :] = a
    a1 = abuf_ref[7:7 + tm, :]
    a2 = abuf_ref[6:6 + tm, :]
    cw = cw_ref[...]
    ac = cw[0:1] * a2 + cw[1:2] * a1 + cw[2:3] * a + cb_ref[...]
    act = (ac * jax.nn.sigmoid(ac) * gate).astype(BF16)
    out_ref[0] = x1 + gt2_ref[0] * jnp.dot(act, wd_ref[...], preferred_element_type=F32)


def _post_attention_ffn(x, o, wo, gt1, g, sh, sc, gt2, wa, wg, cw, cb, wd, tm=256):
    b, l, d = x.shape
    f = wa.shape[1]
    tm = min(tm, l)
    row = lambda i, j: (i, j, 0)
    per_batch = lambda i, j: (i, 0, 0)
    const = lambda i, j: (0, 0)
    return pl.pallas_call(
        functools.partial(_ffn_kernel, tm=tm),
        grid=(b, l // tm),
        in_specs=[
            pl.BlockSpec((1, tm, d), row),
            pl.BlockSpec((1, tm, d), row),
            _resident((d, d), const),
            pl.BlockSpec((1, 1, d), per_batch),
            pl.BlockSpec((1, d), const),
            pl.BlockSpec((1, 1, d), per_batch),
            pl.BlockSpec((1, 1, d), per_batch),
            pl.BlockSpec((1, 1, d), per_batch),
            _resident((d, f), const),
            _resident((d, f), const),
            pl.BlockSpec((3, f), const),
            pl.BlockSpec((1, f), const),
            _resident((f, d), const),
        ],
        out_specs=pl.BlockSpec((1, tm, d), row),
        out_shape=jax.ShapeDtypeStruct((b, l, d), F32),
        scratch_shapes=[pltpu.VMEM((tm + 8, f), F32)],
        compiler_params=_params("arbitrary", "arbitrary"),
        name="post_attn_ffn",
    )(x, o, wo, gt1.reshape(b, 1, d), g.reshape(1, d), sh.reshape(b, 1, d), sc.reshape(b, 1, d),
      gt2.reshape(b, 1, d), wa, wg, cw, cb.reshape(1, f), wd)


def _key_to_float(key):
    bits = jnp.where(key < 0, jnp.int32(-2 ** 31) - key, key)
    return lax.bitcast_convert_type(bits, F32)


def _dsa_kernel(q_ref, qi_ref, wi_ref, kt_ref, va_ref, kit_ref, gq_ref, o_ref,
                s_ref, m_ref, acc_ref, *, tq, tk, top_k, scale, slopes):
    qt = pl.program_id(1)
    n_groups = kt_ref.shape[1]
    group = q_ref.shape[1] // n_groups
    nrep = tk // LANES
    nch = ((qt + 1) * tq + tk - 1) // tk
    q_pos = lax.broadcasted_iota(I32, (tq, tk), 0) + qt * tq
    k_off = lax.broadcasted_iota(I32, (tq, tk), 1)
    kf = float(top_k)

    def rep(x):
        return jnp.tile(x, (1, nrep))

    w = wi_ref[0] * (IDX_HEADS ** -0.5 * IDX_DIM ** -0.5)

    def score_chunk(j, carry):
        kc = kit_ref[0, j]
        sc = jnp.zeros((tq, tk), F32)
        for hh in range(IDX_HEADS):
            rel = jnp.maximum(jnp.dot(qi_ref[0, hh], kc, preferred_element_type=F32), 0.0)
            sc = sc + rel * w[:, hh:hh + 1]
        s_ref[j] = jnp.where(k_off + j * tk <= q_pos, sc, -jnp.inf)
        return carry

    lax.fori_loop(0, nch, score_chunk, 0)

    def count(pred):
        def body(j, c):
            x = jnp.where(pred(s_ref[j], j), 1.0, 0.0)
            for r in range(nrep):
                c = c + x[:, r * LANES:(r + 1) * LANES]
            return c
        c = lax.fori_loop(0, nch, body, jnp.zeros((tq, LANES), F32))
        return jnp.sum(c, axis=-1, keepdims=True)

    def search(_):
        def cond(st):
            return st[3] > 0.0

        def step(st):
            lo, hi, clo, _ = st
            active = (clo != kf) & (hi > lo + 1)
            mid = (lo >> 1) + (hi >> 1) + (lo & hi & 1)
            cmid = rep(_key_to_float(mid))
            cnt = count(lambda s, j: s >= cmid)
            ge = cnt >= kf
            take_lo = active & ge
            take_hi = active & jnp.logical_not(ge)
            lo = jnp.where(take_lo, mid, lo)
            clo = jnp.where(take_lo, cnt, clo)
            hi = jnp.where(take_hi, mid, hi)
            still = (clo != kf) & (hi > lo + 1)
            return lo, hi, clo, jnp.max(jnp.where(still, 1.0, 0.0))

        st0 = (jnp.full((tq, LANES), KEY_NEG_INF, I32), jnp.full((tq, LANES), KEY_POS_INF + 1, I32),
               jnp.full((tq, LANES), 1.0, F32) * (nch * tk).astype(F32), jnp.float32(1.0))
        lo, _, clo, _ = lax.while_loop(cond, step, st0)
        return _key_to_float(lo), clo

    def no_search(_):
        return jnp.full((tq, LANES), -jnp.inf, F32), jnp.full((tq, LANES), kf, F32)

    tau, n_ge = lax.cond((qt + 1) * tq > top_k, search, no_search, 0)
    tau_sel = rep(jnp.maximum(tau, F32_LOWEST))

    tie = (n_ge > kf) & (tau > -jnp.inf)
    any_tie = jnp.max(jnp.where(tie, 1.0, 0.0)) > 0.0

    def mask_plain(_):
        def body(j, carry):
            s_ref[j] = jnp.where(s_ref[j] >= tau_sel, 0.0, NEG_BIG)
            return carry
        lax.fori_loop(0, nch, body, 0)
        return 0

    def mask_ties(_):
        taur = rep(tau)
        need = kf - count(lambda s, j: s > taur)

        def step(_, lohi):
            lo, hi = lohi
            mid = (lo + hi) >> 1
            midr = rep(mid)
            ge = count(lambda s, j: (s == taur) & (k_off + j * tk <= midr)) >= need
            return jnp.where(ge, lo, mid), jnp.where(ge, mid, hi)

        steps = max(1, int(math.ceil(math.log2(s_ref.shape[0] * tk + 1))))
        lo0 = jnp.full((tq, LANES), -1, I32)
        hi0 = jnp.full((tq, LANES), 1, I32) * (nch * tk - 1)
        _, hi = lax.fori_loop(0, steps, step, (lo0, hi0))
        jstar = rep(jnp.where(tie, hi, jnp.int32(2 ** 30)))

        def body(j, carry):
            s = s_ref[j]
            eq_bias = jnp.where(k_off + j * tk <= jstar, 0.0, NEG_BIG)
            s_ref[j] = jnp.where(s > tau_sel, 0.0, jnp.where(s == tau_sel, eq_bias, NEG_BIG))
            return carry
        lax.fori_loop(0, nch, body, 0)
        return 0

    lax.cond(any_tie, mask_ties, mask_plain, 0)

    gq = gq_ref[...]
    rel_pos = (lax.broadcasted_iota(I32, (1, 1, tk), 2) - qt * tq).astype(F32)
    for g in range(n_groups):
        qs = []
        for r in range(group):
            qn = _head_rms(q_ref[0, g * group + r], gq) * (scale * LOG2E)
            qs.append(qn.astype(BF16))
        qb = jnp.concatenate(qs, axis=0)
        slope = jnp.concatenate(
            [jnp.full((1, 1, 1), slopes[g * group + r] * LOG2E, F32) for r in range(group)], axis=0)
        m_ref[...] = jnp.full(m_ref.shape, -jnp.inf, F32)
        acc_ref[...] = jnp.zeros(acc_ref.shape, F32)

        def att_chunk(j, carry, g=g, qb=qb, slope=slope):
            s = jnp.dot(qb, kt_ref[0, g, j], preferred_element_type=F32)
            alibi = slope * (rel_pos + (j * tk).astype(F32))
            s = (s.reshape(group, tq, tk) + alibi + s_ref[j][None]).reshape(group * tq, tk)
            _online_softmax_step(s, m_ref, acc_ref, va_ref[0, g, j], 0)
            return carry

        lax.fori_loop(0, nch, att_chunk, 0)
        acc = acc_ref[0]
        out = acc[:, :HEAD_DIM] / acc[:, HEAD_DIM:HEAD_DIM + 1]
        for r in range(group):
            o_ref[0, g * group + r] = out[r * tq:(r + 1) * tq].astype(o_ref.dtype)


def _dsa_attention(q, q_idx, w_idx, kt, va, kit, g_q, tq, tk, top_k):
    b, hq, l, dh = q.shape
    n_groups = kt.shape[1]
    nk = l // tk
    slopes = tuple(2.0 ** (-8.0 * (i + 1) / hq) for i in range(hq))
    rows = (hq // n_groups) * tq
    return pl.pallas_call(
        functools.partial(_dsa_kernel, tq=tq, tk=tk, top_k=top_k, scale=dh ** -0.5, slopes=slopes),
        grid=(b, l // tq),
        in_specs=[
            pl.BlockSpec((1, hq, tq, dh), lambda i, j: (i, 0, j, 0)),
            pl.BlockSpec((1, IDX_HEADS, tq, IDX_DIM), lambda i, j: (i, 0, j, 0)),
            pl.BlockSpec((1, tq, IDX_HEADS), lambda i, j: (i, j, 0)),
            _resident((1, n_groups, nk, dh, tk), lambda i, j: (i, 0, 0, 0, 0)),
            _resident((1, n_groups, nk, tk, LANES), lambda i, j: (i, 0, 0, 0, 0)),
            _resident((1, nk, IDX_DIM, tk), lambda i, j: (i, 0, 0, 0)),
            pl.BlockSpec((1, dh), lambda i, j: (0, 0)),
        ],
        out_specs=pl.BlockSpec((1, hq, tq, dh), lambda i, j: (i, 0, j, 0)),
        out_shape=jax.ShapeDtypeStruct((b, hq, l, dh), BF16),
        scratch_shapes=[
            pltpu.VMEM((nk, tq, tk), F32),
            pltpu.VMEM((1, rows, LANES), F32),
            pltpu.VMEM((1, rows, LANES), F32),
        ],
        compiler_params=_params("arbitrary", "arbitrary"),
        name="dsa_attn",
    )(q, q_idx, w_idx, kt, va, kit, g_q.reshape(1, dh))


def _heads_major(a, n_heads):
    b, l, _ = a.shape
    return a.reshape(b, l, n_heads, -1).transpose(0, 2, 1, 3)


def _keys_transposed(a, n_heads, t):
    b, l, _ = a.shape
    return a.reshape(b, l // t, t, n_heads, -1).transpose(0, 3, 1, 4, 2)


def _values_chunked(a, n_heads, t):
    b, l, _ = a.shape
    return a.reshape(b, l // t, t, n_heads, -1).transpose(0, 3, 1, 2, 4)


def _merge_heads(o):
    b, h, l, dh = o.shape
    return o.transpose(0, 2, 1, 3).reshape(b, l, h * dh)


def _pad_cols(w, mult=LANES):
    n = w.shape[1]
    return jnp.pad(w, ((0, 0), (0, _round_up(n, mult) - n)))


def kernel(x, c, w_ada, b_ada, g_norm_mix, g_norm_ffn, fox_w_in, fox_b_f, fox_g_q, fox_g_k, fox_w_out,
           dsa_w_in, dsa_g_q, dsa_g_k, dsa_g_kidx, dsa_w_out, ffn_w_up, ffn_conv_w, ffn_conv_b, ffn_w_down):
    b, l, d = x.shape
    dh = HEAD_DIM
    n_heads = d // dh
    d_ff = ffn_w_down.shape[1]
    top_k = min(TOPK_MAX, l // 4)
    tq = min(Q_TILE, l)
    tk = min(K_TILE, l)

    mod = _adaln(c, w_ada, b_ada)

    def ffn(i, xx, o, w_out):
        sh2, sc2, gt2 = mod[i, :, 3 * d:4 * d], mod[i, :, 4 * d:5 * d], mod[i, :, 5 * d:6 * d]
        gt1 = mod[i, :, 2 * d:3 * d]
        wup = ffn_w_up[i].astype(BF16)
        return _post_attention_ffn(xx, o, w_out.astype(BF16), gt1, g_norm_ffn[i], sh2, sc2, gt2,
                                   wup[:, :d_ff], wup[:, d_ff:], ffn_conv_w[i], ffn_conv_b[i],
                                   ffn_w_down[i].astype(BF16))

    proj = _inproj(x, g_norm_mix[0], mod[0, :, 0:d], mod[0, :, d:2 * d], _pad_cols(fox_w_in[0]).astype(BF16))
    q = _heads_major(proj[..., 0:d], n_heads)
    kt_raw = _keys_transposed(proj[..., d:2 * d], n_heads, tk)
    v = _values_chunked(proj[..., 2 * d:3 * d], n_heads, tk).astype(BF16)
    f_t = proj[..., 3 * d:3 * d + n_heads].transpose(0, 2, 1)
    kt = _knorm(kt_raw.reshape(-1, dh, tk), fox_g_k[0]).reshape(kt_raw.shape)
    cum = _cumgate(f_t, fox_b_f[0]).reshape(b, n_heads, l // tk, 1, tk)
    o = _fox_attention(q, kt, v, cum, fox_g_q[0], tq, tk)
    x = ffn(0, x, _merge_heads(o), fox_w_out[0])

    g_kv = DSA_KV_HEADS
    kvw = g_kv * dh
    proj = _inproj(x, g_norm_mix[1], mod[1, :, 0:d], mod[1, :, d:2 * d], _pad_cols(dsa_w_in[0]).astype(BF16))
    o0, o1, o2 = d, d + kvw, d + 2 * kvw
    o3 = o2 + IDX_HEADS * IDX_DIM
    o4 = o3 + IDX_DIM
    q = _heads_major(proj[..., 0:o0], n_heads)
    kt_raw = _keys_transposed(proj[..., o0:o1], g_kv, tk)
    v = _values_chunked(proj[..., o1:o2], g_kv, tk).astype(BF16)
    ones = jnp.zeros(v.shape[:-1] + (LANES - dh,), BF16).at[..., 0].set(1)
    va = jnp.concatenate([v, ones], axis=-1)
    q_idx = _heads_major(proj[..., o2:o3], IDX_HEADS).astype(BF16)
    kit_raw = _keys_transposed(proj[..., o3:o4], 1, tk)
    w_idx = proj[..., o4:o4 + IDX_HEADS]
    kt = _knorm(kt_raw.reshape(-1, dh, tk), dsa_g_k[0]).reshape(kt_raw.shape)
    kit = _knorm(kit_raw.reshape(-1, IDX_DIM, tk), dsa_g_kidx[0]).reshape(b, l // tk, IDX_DIM, tk)
    o = _dsa_attention(q, q_idx, w_idx, kt, va, kit, dsa_g_q[0], tq, tk, top_k)
    x = ffn(1, x, _merge_heads(o), dsa_w_out[0])
    return x
```

```python
import functools
import math

import jax
import jax.numpy as jnp
from jax import lax
from jax.experimental import pallas as pl
from jax.experimental.pallas import tpu as pltpu

F32 = jnp.float32
BF16 = jnp.bfloat16
I32 = jnp.int32

HEAD_DIM = 64
DSA_KV_HEADS = 4
IDX_HEADS = 8
IDX_DIM = 64
TOPK_MAX = 256
EPS = 1e-6
LOG2E = 1.4426950408889634
LANES = 128
VMEM_LIMIT = 56 * 1024 * 1024
NEG_BIG = -1e30
F32_LOWEST = -3.4028234663852886e38
KEY_NEG_INF = -0x7F800000
KEY_POS_INF = 0x7F800000
Q_TILE = 256
K_TILE = 512


def _params(*sem):
    return pltpu.CompilerParams(dimension_semantics=sem, vmem_limit_bytes=VMEM_LIMIT)


def _resident(shape, index_map):
    return pl.BlockSpec(shape, index_map, pipeline_mode=pl.Buffered(1))


def _round_up(n, m):
    return (n + m - 1) // m * m


def _adaln_kernel(c_ref, w_ref, b_ref, o_ref):
    c = c_ref[...]
    ca = c * jax.nn.sigmoid(c)
    o_ref[0] = jnp.dot(ca, w_ref[0], preferred_element_type=F32) + b_ref[0]


def _adaln(c, w_ada, b_ada):
    depth, d, n = w_ada.shape
    b = c.shape[0]
    rows = _round_up(b, 8)
    c_pad = jnp.pad(c, ((0, rows - b), (0, 0)))
    tn = 1536
    out = pl.pallas_call(
        _adaln_kernel,
        grid=(depth, n // tn),
        in_specs=[
            pl.BlockSpec((rows, d), lambda i, j: (0, 0)),
            pl.BlockSpec((1, d, tn), lambda i, j: (i, 0, j)),
            pl.BlockSpec((1, 1, tn), lambda i, j: (i, 0, j)),
        ],
        out_specs=pl.BlockSpec((1, rows, tn), lambda i, j: (i, 0, j)),
        out_shape=jax.ShapeDtypeStruct((depth, rows, n), F32),
        compiler_params=_params("arbitrary", "arbitrary"),
        name="adaln",
    )(c_pad, w_ada, b_ada.reshape(depth, 1, n))
    return out[:, :b]


def _modulated(x, g, sh, sc):
    ms = jnp.mean(x * x, axis=-1, keepdims=True)
    return x * lax.rsqrt(ms + EPS) * (g * (1.0 + sc)) + sh


def _inproj_kernel(x_ref, g_ref, sh_ref, sc_ref, w_ref, o_ref):
    h = _modulated(x_ref[0], g_ref[...], sh_ref[0], sc_ref[0])
    o_ref[0] = jnp.dot(h.astype(BF16), w_ref[...], preferred_element_type=F32)


def _inproj(x, g, sh, sc, w_bf16, tm=512):
    b, l, d = x.shape
    n = w_bf16.shape[1]
    tm = min(tm, l)
    return pl.pallas_call(
        _inproj_kernel,
        grid=(b, l // tm),
        in_specs=[
            pl.BlockSpec((1, tm, d), lambda i, j: (i, j, 0)),
            pl.BlockSpec((1, d), lambda i, j: (0, 0)),
            pl.BlockSpec((1, 1, d), lambda i, j: (i, 0, 0)),
            pl.BlockSpec((1, 1, d), lambda i, j: (i, 0, 0)),
            _resident((d, n), lambda i, j: (0, 0)),
        ],
        out_specs=pl.BlockSpec((1, tm, n), lambda i, j: (i, j, 0)),
        out_shape=jax.ShapeDtypeStruct((b, l, n), F32),
        compiler_params=_params("arbitrary", "arbitrary"),
        name="inproj",
    )(x, g.reshape(1, d), sh.reshape(b, 1, d), sc.reshape(b, 1, d), w_bf16)


def _kt_prep_kernel(x_ref, g_ref, o_ref, *, heads):
    xt = x_ref[0].T
    for hh in range(heads):
        xh = xt[hh * HEAD_DIM:(hh + 1) * HEAD_DIM]
        ms = jnp.mean(xh * xh, axis=0, keepdims=True)
        o_ref[0, hh, 0] = (xh * lax.rsqrt(ms + EPS) * g_ref[...]).astype(o_ref.dtype)


def _kt_prep(proj, g, col0, n_heads, tk):
    b, l, _ = proj.shape
    heads = min(2, n_heads)
    cb0 = col0 // LANES
    return pl.pallas_call(
        functools.partial(_kt_prep_kernel, heads=heads),
        grid=(b, n_heads // heads, l // tk),
        in_specs=[
            pl.BlockSpec((1, tk, LANES), lambda i, j, k: (i, k, cb0 + j)),
            pl.BlockSpec((HEAD_DIM, 1), lambda i, j, k: (0, 0)),
        ],
        out_specs=pl.BlockSpec((1, heads, 1, HEAD_DIM, tk), lambda i, j, k: (i, j, k, 0, 0)),
        out_shape=jax.ShapeDtypeStruct((b, n_heads, l // tk, HEAD_DIM, tk), BF16),
        compiler_params=_params("arbitrary", "arbitrary", "arbitrary"),
        name="kt_prep",
    )(proj, g.reshape(HEAD_DIM, 1))


def _va_prep_kernel(x_ref, o_ref):
    x = x_ref[0]
    lane = lax.broadcasted_iota(I32, x.shape, 1)
    ones_col = jnp.where(lane == HEAD_DIM, 1.0, 0.0)
    o_ref[0, 0, 0] = jnp.where(lane < HEAD_DIM, x, ones_col).astype(o_ref.dtype)
    o_ref[0, 1, 0] = jnp.where(lane < HEAD_DIM, pltpu.roll(x, HEAD_DIM, axis=1), ones_col).astype(o_ref.dtype)


def _va_prep(proj, col0, n_heads, tk):
    b, l, _ = proj.shape
    cb0 = col0 // LANES
    return pl.pallas_call(
        _va_prep_kernel,
        grid=(b, n_heads // 2, l // tk),
        in_specs=[pl.BlockSpec((1, tk, LANES), lambda i, j, k: (i, k, cb0 + j))],
        out_specs=pl.BlockSpec((1, 2, 1, tk, LANES), lambda i, j, k: (i, j, k, 0, 0)),
        out_shape=jax.ShapeDtypeStruct((b, n_heads, l // tk, tk, LANES), BF16),
        compiler_params=_params("arbitrary", "arbitrary", "arbitrary"),
        name="va_prep",
    )(proj)


def _split3(x):
    hi = x.astype(BF16)
    r = x - hi.astype(F32)
    mid = r.astype(BF16)
    lo = (r - mid.astype(F32)).astype(BF16)
    return hi, mid, lo


def _cumgate_kernel(f_ref, b_ref, o_ref, *, cw):
    x = f_ref[0] + b_ref[...]
    lf = jnp.minimum(x, 0.0) - jnp.log1p(jnp.exp(-jnp.abs(x)))
    h, l = lf.shape
    row = lax.broadcasted_iota(I32, (cw, cw), 0)
    col = lax.broadcasted_iota(I32, (cw, cw), 1)
    tri = jnp.where(row <= col, 1.0, 0.0).astype(BF16)
    carry = jnp.zeros((h, 1), F32)
    for c in range(l // cw):
        hi, mid, lo = _split3(lf[:, c * cw:(c + 1) * cw])
        cs = (jnp.dot(hi, tri, preferred_element_type=F32)
              + jnp.dot(mid, tri, preferred_element_type=F32)
              + jnp.dot(lo, tri, preferred_element_type=F32)) + carry
        o_ref[0, :, c * cw:(c + 1) * cw] = cs * LOG2E
        carry = cs[:, cw - 1:cw]


def _cumgate(f_t, b_f):
    b, h, l = f_t.shape
    cw = min(256, l)
    return pl.pallas_call(
        functools.partial(_cumgate_kernel, cw=cw),
        grid=(b,),
        in_specs=[
            pl.BlockSpec((1, h, l), lambda i: (i, 0, 0)),
            pl.BlockSpec((h, 1), lambda i: (0, 0)),
        ],
        out_specs=pl.BlockSpec((1, h, l), lambda i: (i, 0, 0)),
        out_shape=jax.ShapeDtypeStruct((b, h, l), F32),
        compiler_params=_params("arbitrary"),
        name="cumgate",
    )(f_t, b_f.reshape(h, 1))


def _head_rms(q, g):
    ms = jnp.mean(q * q, axis=-1, keepdims=True)
    return q * lax.rsqrt(ms + EPS) * g


def _online_softmax_step(s, m_ref, acc_ref, va, idx):
    m_prev = m_ref[idx]
    m_new = jnp.maximum(m_prev, jnp.max(s, axis=-1, keepdims=True))
    p = jnp.exp2(s - jnp.tile(m_new, (1, s.shape[-1] // LANES)))
    alpha = jnp.exp2(m_prev - m_new)
    acc_ref[idx] = alpha * acc_ref[idx] + jnp.dot(p.astype(BF16), va, preferred_element_type=F32)
    m_ref[idx] = m_new


def _fox_kernel(q_ref, kt_ref, va_ref, c_ref, gq_ref, o_ref, m_ref, acc_ref, *, tq, tk, hb, scale):
    qi = pl.program_id(2)
    gq = gq_ref[...]
    qbs = [(_head_rms(q_ref[0, :, h * HEAD_DIM:(h + 1) * HEAD_DIM], gq) * (scale * LOG2E)).astype(BF16)
           for h in range(hb)]
    m_ref[...] = jnp.full(m_ref.shape, -jnp.inf, F32)
    acc_ref[...] = jnp.zeros(acc_ref.shape, F32)
    nch = ((qi + 1) * tq + tk - 1) // tk

    def chunk(j, last):
        ss = []
        for h in range(hb):
            s = jnp.dot(qbs[h], kt_ref[0, h, j], preferred_element_type=F32) - c_ref[0, h, j]
            if last:
                row = lax.broadcasted_iota(I32, (tq, tk), 0) + qi * tq
                col = lax.broadcasted_iota(I32, (tq, tk), 1) + j * tk
                s = jnp.where(col <= row, s, -jnp.inf)
            ss.append(s)
        for h in range(hb):
            _online_softmax_step(ss[h], m_ref, acc_ref, va_ref[0, h, j], h)

    def body(j, carry):
        chunk(j, False)
        return carry

    lax.fori_loop(0, nch - 1, body, 0)
    chunk(nch - 1, True)
    for h in range(hb):
        acc = acc_ref[h]
        o_ref[0, :, h * HEAD_DIM:(h + 1) * HEAD_DIM] = (
            acc[:, :HEAD_DIM] / acc[:, HEAD_DIM:HEAD_DIM + 1]).astype(o_ref.dtype)


def _fox_attention(proj, kt, va, cum, g_q, tq, tk, hb=4):
    b, l, _ = proj.shape
    h, nk, dh = kt.shape[1], kt.shape[2], kt.shape[3]
    hb = math.gcd(hb, h)
    return pl.pallas_call(
        functools.partial(_fox_kernel, tq=tq, tk=tk, hb=hb, scale=dh ** -0.5),
        grid=(b, h // hb, l // tq),
        in_specs=[
            pl.BlockSpec((1, tq, hb * dh), lambda i, j, k: (i, k, j)),
            _resident((1, hb, nk, dh, tk), lambda i, j, k: (i, j, 0, 0, 0)),
            _resident((1, hb, nk, tk, LANES), lambda i, j, k: (i, j, 0, 0, 0)),
            _resident((1, hb, nk, 1, tk), lambda i, j, k: (i, j, 0, 0, 0)),
            pl.BlockSpec((1, dh), lambda i, j, k: (0, 0)),
        ],
        out_specs=pl.BlockSpec((1, tq, hb * dh), lambda i, j, k: (i, k, j)),
        out_shape=jax.ShapeDtypeStruct((b, l, h * dh), BF16),
        scratch_shapes=[
            pltpu.VMEM((hb, tq, LANES), F32),
            pltpu.VMEM((hb, tq, LANES), F32),
        ],
        compiler_params=_params("arbitrary", "arbitrary", "arbitrary"),
        name="fox_attn",
    )(proj, kt, va, cum, g_q.reshape(1, dh))


def _ffn_kernel(x_ref, o_ref, wo_ref, gt1_ref, g_ref, sh_ref, sc_ref, gt2_ref,
                wa_ref, wg_ref, cw_ref, cb_ref, wd_ref, out_ref, abuf_ref, *, tm):
    i = pl.program_id(1)
    y = jnp.dot(o_ref[0], wo_ref[...], preferred_element_type=F32)
    x1 = x_ref[0] + gt1_ref[0] * y
    h = _modulated(x1, g_ref[...], sh_ref[0], sc_ref[0]).astype(BF16)
    a = jnp.dot(h, wa_ref[...], preferred_element_type=F32)
    gate = jnp.dot(h, wg_ref[...], preferred_element_type=F32)

    @pl.when(i == 0)
    def _():
        abuf_ref[0:8, :] = jnp.zeros((8, abuf_ref.shape[1]), F32)

    @pl.when(i > 0)
    def _():
        abuf_ref[0:8, :] = abuf_ref[tm:tm + 8, :]

    abuf_ref[8:8 + tm, :] = a
    a1 = abuf_ref[7:7 + tm, :]
    a2 = abuf_ref[6:6 + tm, :]
    cw = cw_ref[...]
    ac = cw[0:1] * a2 + cw[1:2] * a1 + cw[2:3] * a + cb_ref[...]
    act = (ac * jax.nn.sigmoid(ac) * gate).astype(BF16)
    out_ref[0] = x1 + gt2_ref[0] * jnp.dot(act, wd_ref[...], preferred_element_type=F32)


def _post_attention_ffn(x, o, wo, gt1, g, sh, sc, gt2, wa, wg, cw, cb, wd, tm=256):
    b, l, d = x.shape
    f = wa.shape[1]
    tm = min(tm, l)
    row = lambda i, j: (i, j, 0)
    per_batch = lambda i, j: (i, 0, 0)
    const = lambda i, j: (0, 0)
    return pl.pallas_call(
        functools.partial(_ffn_kernel, tm=tm),
        grid=(b, l // tm),
        in_specs=[
            pl.BlockSpec((1, tm, d), row),
            pl.BlockSpec((1, tm, d), row),
            _resident((d, d), const),
            pl.BlockSpec((1, 1, d), per_batch),
            pl.BlockSpec((1, d), const),
            pl.BlockSpec((1, 1, d), per_batch),
            pl.BlockSpec((1, 1, d), per_batch),
            pl.BlockSpec((1, 1, d), per_batch),
            _resident((d, f), const),
            _resident((d, f), const),
            pl.BlockSpec((3, f), const),
            pl.BlockSpec((1, f), const),
            _resident((f, d), const),
        ],
        out_specs=pl.BlockSpec((1, tm, d), row),
        out_shape=jax.ShapeDtypeStruct((b, l, d), F32),
        scratch_shapes=[pltpu.VMEM((tm + 8, f), F32)],
        compiler_params=_params("arbitrary", "arbitrary"),
        name="post_attn_ffn",
    )(x, o, wo, gt1.reshape(b, 1, d), g.reshape(1, d), sh.reshape(b, 1, d), sc.reshape(b, 1, d),
      gt2.reshape(b, 1, d), wa, wg, cw, cb.reshape(1, f), wd)


def _key_to_float(key):
    bits = jnp.where(key < 0, jnp.int32(-2 ** 31) - key, key)
    return lax.bitcast_convert_type(bits, F32)


def _dsa_kernel(q_ref, qi_ref, wi_ref, kt_ref, va_ref, kit_ref, gq_ref, o_ref,
                s_ref, m_ref, acc_ref, *, tq, tk, top_k, scale, slopes):
    qt = pl.program_id(1)
    n_groups = kt_ref.shape[1]
    group = q_ref.shape[2] // HEAD_DIM // n_groups
    nrep = tk // LANES
    nch = ((qt + 1) * tq + tk - 1) // tk
    q_pos = lax.broadcasted_iota(I32, (tq, tk), 0) + qt * tq
    k_off = lax.broadcasted_iota(I32, (tq, tk), 1)
    kf = float(top_k)

    def rep(x):
        return jnp.tile(x, (1, nrep))

    w = wi_ref[0][:, IDX_DIM:IDX_DIM + IDX_HEADS] * (IDX_HEADS ** -0.5 * IDX_DIM ** -0.5)
    qis = [qi_ref[0, :, hh * IDX_DIM:(hh + 1) * IDX_DIM].astype(BF16) for hh in range(IDX_HEADS)]

    def score_chunk(j, carry):
        kc = kit_ref[0, 0, j]
        sc = jnp.zeros((tq, tk), F32)
        for hh in range(IDX_HEADS):
            rel = jnp.maximum(jnp.dot(qis[hh], kc, preferred_element_type=F32), 0.0)
            sc = sc + rel * w[:, hh:hh + 1]
        s_ref[j] = jnp.where(k_off + j * tk <= q_pos, sc, -jnp.inf)
        return carry

    lax.fori_loop(0, nch, score_chunk, 0)

    def count(pred):
        def body(j, c):
            x = jnp.where(pred(s_ref[j], j), 1.0, 0.0)
            for r in range(nrep):
                c = c + x[:, r * LANES:(r + 1) * LANES]
            return c
        c = lax.fori_loop(0, nch, body, jnp.zeros((tq, LANES), F32))
        return jnp.sum(c, axis=-1, keepdims=True)

    def search(_):
        def cond(st):
            return st[3] > 0.0

        def step(st):
            lo, hi, clo, _ = st
            active = (clo != kf) & (hi > lo + 1)
            mid = (lo >> 1) + (hi >> 1) + (lo & hi & 1)
            cmid = rep(_key_to_float(mid))
            cnt = count(lambda s, j: s >= cmid)
            ge = cnt >= kf
            take_lo = active & ge
            take_hi = active & jnp.logical_not(ge)
            lo = jnp.where(take_lo, mid, lo)
            clo = jnp.where(take_lo, cnt, clo)
            hi = jnp.where(take_hi, mid, hi)
            still = (clo != kf) & (hi > lo + 1)
            return lo, hi, clo, jnp.max(jnp.where(still, 1.0, 0.0))

        st0 = (jnp.full((tq, LANES), KEY_NEG_INF, I32), jnp.full((tq, LANES), KEY_POS_INF + 1, I32),
               jnp.full((tq, LANES), 1.0, F32) * (nch * tk).astype(F32), jnp.float32(1.0))
        lo, _, clo, _ = lax.while_loop(cond, step, st0)
        return _key_to_float(lo), clo

    def no_search(_):
        return jnp.full((tq, LANES), -jnp.inf, F32), jnp.full((tq, LANES), kf, F32)

    tau, n_ge = lax.cond((qt + 1) * tq > top_k, search, no_search, 0)
    tau_sel = rep(jnp.maximum(tau, F32_LOWEST))

    tie = (n_ge > kf) & (tau > -jnp.inf)
    any_tie = jnp.max(jnp.where(tie, 1.0, 0.0)) > 0.0

    def mask_plain(_):
        def body(j, carry):
            s_ref[j] = jnp.where(s_ref[j] >= tau_sel, 0.0, NEG_BIG)
            return carry
        lax.fori_loop(0, nch, body, 0)
        return 0

    def mask_ties(_):
        taur = rep(tau)
        need = kf - count(lambda s, j: s > taur)

        def step(_, lohi):
            lo, hi = lohi
            mid = (lo + hi) >> 1
            midr = rep(mid)
            ge = count(lambda s, j: (s == taur) & (k_off + j * tk <= midr)) >= need
            return jnp.where(ge, lo, mid), jnp.where(ge, mid, hi)

        steps = max(1, int(math.ceil(math.log2(s_ref.shape[0] * tk + 1))))
        lo0 = jnp.full((tq, LANES), -1, I32)
        hi0 = jnp.full((tq, LANES), 1, I32) * (nch * tk - 1)
        _, hi = lax.fori_loop(0, steps, step, (lo0, hi0))
        jstar = rep(jnp.where(tie, hi, jnp.int32(2 ** 30)))

        def body(j, carry):
            s = s_ref[j]
            eq_bias = jnp.where(k_off + j * tk <= jstar, 0.0, NEG_BIG)
            s_ref[j] = jnp.where(s > tau_sel, 0.0, jnp.where(s == tau_sel, eq_bias, NEG_BIG))
            return carry
        lax.fori_loop(0, nch, body, 0)
        return 0

    lax.cond(any_tie, mask_ties, mask_plain, 0)

    gq = gq_ref[...]
    rel_pos = (lax.broadcasted_iota(I32, (1, 1, tk), 2) - qt * tq).astype(F32)
    for g in range(n_groups):
        qs = []
        for r in range(group):
            hq = g * group + r
            qn = _head_rms(q_ref[0, :, hq * HEAD_DIM:(hq + 1) * HEAD_DIM], gq) * (scale * LOG2E)
            qs.append(qn.astype(BF16))
        qb = jnp.concatenate(qs, axis=0)
        slope = jnp.concatenate(
            [jnp.full((1, 1, 1), slopes[g * group + r] * LOG2E, F32) for r in range(group)], axis=0)
        m_ref[...] = jnp.full(m_ref.shape, -jnp.inf, F32)
        acc_ref[...] = jnp.zeros(acc_ref.shape, F32)

        def att_chunk(j, carry, g=g, qb=qb, slope=slope):
            s = jnp.dot(qb, kt_ref[0, g, j], preferred_element_type=F32)
            alibi = slope * (rel_pos + (j * tk).astype(F32))
            s = (s.reshape(group, tq, tk) + alibi + s_ref[j][None]).reshape(group * tq, tk)
            _online_softmax_step(s, m_ref, acc_ref, va_ref[0, g, j], 0)
            return carry

        lax.fori_loop(0, nch, att_chunk, 0)
        acc = acc_ref[0]
        out = acc[:, :HEAD_DIM] / acc[:, HEAD_DIM:HEAD_DIM + 1]
        for r in range(group):
            hq = g * group + r
            o_ref[0, :, hq * HEAD_DIM:(hq + 1) * HEAD_DIM] = out[r * tq:(r + 1) * tq].astype(o_ref.dtype)


def _dsa_attention(proj, kt, va, kit, g_q, n_q_heads, qi_col0, wi_col0, tq, tk, top_k):
    b, l, _ = proj.shape
    dh = HEAD_DIM
    n_groups, nk = kt.shape[1], kt.shape[2]
    qw = n_q_heads * dh
    iw = IDX_HEADS * IDX_DIM
    assert qi_col0 % iw == 0 and wi_col0 % LANES == IDX_DIM
    slopes = tuple(2.0 ** (-8.0 * (i + 1) / n_q_heads) for i in range(n_q_heads))
    rows = (n_q_heads // n_groups) * tq
    return pl.pallas_call(
        functools.partial(_dsa_kernel, tq=tq, tk=tk, top_k=top_k, scale=dh ** -0.5, slopes=slopes),
        grid=(b, l // tq),
        in_specs=[
            pl.BlockSpec((1, tq, qw), lambda i, j: (i, j, 0)),
            pl.BlockSpec((1, tq, iw), lambda i, j: (i, j, qi_col0 // iw)),
            pl.BlockSpec((1, tq, LANES), lambda i, j: (i, j, wi_col0 // LANES)),
            _resident((1, n_groups, nk, dh, tk), lambda i, j: (i, 0, 0, 0, 0)),
            _resident((1, n_groups, nk, tk, LANES), lambda i, j: (i, 0, 0, 0, 0)),
            _resident((1, 1, nk, IDX_DIM, tk), lambda i, j: (i, 0, 0, 0, 0)),
            pl.BlockSpec((1, dh), lambda i, j: (0, 0)),
        ],
        out_specs=pl.BlockSpec((1, tq, qw), lambda i, j: (i, j, 0)),
        out_shape=jax.ShapeDtypeStruct((b, l, qw), BF16),
        scratch_shapes=[
            pltpu.VMEM((nk, tq, tk), F32),
            pltpu.VMEM((1, rows, LANES), F32),
            pltpu.VMEM((1, rows, LANES), F32),
        ],
        compiler_params=_params("arbitrary", "arbitrary"),
        name="dsa_attn",
    )(proj, proj, proj, kt, va, kit, g_q.reshape(1, dh))


def _pad_cols(w, mult=LANES):
    n = w.shape[1]
    return jnp.pad(w, ((0, 0), (0, _round_up(n, mult) - n)))


def kernel(x, c, w_ada, b_ada, g_norm_mix, g_norm_ffn, fox_w_in, fox_b_f, fox_g_q, fox_g_k, fox_w_out,
           dsa_w_in, dsa_g_q, dsa_g_k, dsa_g_kidx, dsa_w_out, ffn_w_up, ffn_conv_w, ffn_conv_b, ffn_w_down):
    b, l, d = x.shape
    dh = HEAD_DIM
    n_heads = d // dh
    d_ff = ffn_w_down.shape[1]
    top_k = min(TOPK_MAX, l // 4)
    tq = min(Q_TILE, l)
    tk = min(K_TILE, l)

    mod = _adaln(c, w_ada, b_ada)

    def ffn(i, xx, o, w_out):
        sh2, sc2, gt2 = mod[i, :, 3 * d:4 * d], mod[i, :, 4 * d:5 * d], mod[i, :, 5 * d:6 * d]
        gt1 = mod[i, :, 2 * d:3 * d]
        wup = ffn_w_up[i].astype(BF16)
        return _post_attention_ffn(xx, o, w_out.astype(BF16), gt1, g_norm_ffn[i], sh2, sc2, gt2,
                                   wup[:, :d_ff], wup[:, d_ff:], ffn_conv_w[i], ffn_conv_b[i],
                                   ffn_w_down[i].astype(BF16))

    proj = _inproj(x, g_norm_mix[0], mod[0, :, 0:d], mod[0, :, d:2 * d], _pad_cols(fox_w_in[0]).astype(BF16))
    kt = _kt_prep(proj, fox_g_k[0], d, n_heads, tk)
    va = _va_prep(proj, 2 * d, n_heads, tk)
    f_t = proj[..., 3 * d:3 * d + n_heads].transpose(0, 2, 1)
    cum = _cumgate(f_t, fox_b_f[0]).reshape(b, n_heads, l // tk, 1, tk)
    o = _fox_attention(proj, kt, va, cum, fox_g_q[0], tq, tk)
    x = ffn(0, x, o, fox_w_out[0])

    g_kv = DSA_KV_HEADS
    kvw = g_kv * dh
    proj = _inproj(x, g_norm_mix[1], mod[1, :, 0:d], mod[1, :, d:2 * d], _pad_cols(dsa_w_in[0]).astype(BF16))
    o0, o1, o2 = d, d + kvw, d + 2 * kvw
    o3 = o2 + IDX_HEADS * IDX_DIM
    o4 = o3 + IDX_DIM
    kt = _kt_prep(proj, dsa_g_k[0], o0, g_kv, tk)
    va = _va_prep(proj, o1, g_kv, tk)
    kit = _kt_prep(proj, dsa_g_kidx[0], o3, 1, tk)
    o = _dsa_attention(proj, kt, va, kit, dsa_g_q[0], n_heads, o2, o4, tq, tk, top_k)
    x = ffn(1, x, o, dsa_w_out[0])
    return x
```

```python
import functools
import math

import jax
import jax.numpy as jnp
import ml_dtypes
import numpy as np
from jax import lax
from jax.experimental import pallas as pl
from jax.experimental.pallas import tpu as pltpu

F32 = jnp.float32
BF16 = jnp.bfloat16
I32 = jnp.int32

HEAD_DIM = 64
DSA_KV_HEADS = 4
IDX_HEADS = 8
IDX_DIM = 64
TOPK_MAX = 256
EPS = 1e-6
LOG2E = 1.4426950408889634
LANES = 128
SUBLANES = 8
VMEM_LIMIT = 56 * 1024 * 1024
NEG_BIG = -1e30
F32_LOWEST = -3.4028234663852886e38
KEY_NEG_INF = -0x7F800000
KEY_POS_INF = 0x7F800000
Q_TILE = 256
K_TILE = 512
AUG_ROWS = SUBLANES
VA_ROWS = 80
POS_SPLIT = 256


def _params(*sem):
    return pltpu.CompilerParams(dimension_semantics=sem, vmem_limit_bytes=VMEM_LIMIT)


def _resident(shape, index_map):
    return pl.BlockSpec(shape, index_map, pipeline_mode=pl.Buffered(1))


def _round_up(n, m):
    return (n + m - 1) // m * m


def _adaln_kernel(c_ref, w_ref, b_ref, o_ref):
    c = c_ref[...]
    ca = c * jax.nn.sigmoid(c)
    o_ref[0] = jnp.dot(ca, w_ref[0], preferred_element_type=F32) + b_ref[0]


def _adaln(c, w_ada, b_ada):
    depth, d, n = w_ada.shape
    b = c.shape[0]
    rows = _round_up(b, 8)
    c_pad = jnp.pad(c, ((0, rows - b), (0, 0)))
    tn = 1536
    out = pl.pallas_call(
        _adaln_kernel,
        grid=(depth, n // tn),
        in_specs=[
            pl.BlockSpec((rows, d), lambda i, j: (0, 0)),
            pl.BlockSpec((1, d, tn), lambda i, j: (i, 0, j)),
            pl.BlockSpec((1, 1, tn), lambda i, j: (i, 0, j)),
        ],
        out_specs=pl.BlockSpec((1, rows, tn), lambda i, j: (i, 0, j)),
        out_shape=jax.ShapeDtypeStruct((depth, rows, n), F32),
        compiler_params=_params("arbitrary", "arbitrary"),
        name="adaln",
    )(c_pad, w_ada, b_ada.reshape(depth, 1, n))
    return out[:, :b]


def _modulated(x, g, sh, sc):
    ms = jnp.mean(x * x, axis=-1, keepdims=True)
    return x * lax.rsqrt(ms + EPS) * (g * (1.0 + sc)) + sh


def _inproj_kernel(x_ref, g_ref, sh_ref, sc_ref, w_ref, o_ref):
    h = _modulated(x_ref[0], g_ref[...], sh_ref[0], sc_ref[0])
    o_ref[0] = jnp.dot(h.astype(BF16), w_ref[...], preferred_element_type=F32)


def _inproj(x, g, sh, sc, w_bf16, tm=512):
    b, l, d = x.shape
    n = w_bf16.shape[1]
    tm = min(tm, l)
    return pl.pallas_call(
        _inproj_kernel,
        grid=(b, l // tm),
        in_specs=[
            pl.BlockSpec((1, tm, d), lambda i, j: (i, j, 0)),
            pl.BlockSpec((1, d), lambda i, j: (0, 0)),
            pl.BlockSpec((1, 1, d), lambda i, j: (i, 0, 0)),
            pl.BlockSpec((1, 1, d), lambda i, j: (i, 0, 0)),
            _resident((d, n), lambda i, j: (0, 0)),
        ],
        out_specs=pl.BlockSpec((1, tm, n), lambda i, j: (i, j, 0)),
        out_shape=jax.ShapeDtypeStruct((b, l, n), F32),
        compiler_params=_params("arbitrary", "arbitrary"),
        name="inproj",
    )(x, g.reshape(1, d), sh.reshape(b, 1, d), sc.reshape(b, 1, d), w_bf16)


def _split3(x):
    hi = x.astype(BF16)
    r = x - hi.astype(F32)
    mid = r.astype(BF16)
    lo = (r - mid.astype(F32)).astype(BF16)
    return hi, mid, lo


def _cumgate_kernel(f_ref, b_ref, o_ref, *, cw):
    x = f_ref[0] + b_ref[...]
    lf = jnp.minimum(x, 0.0) - jnp.log1p(jnp.exp(-jnp.abs(x)))
    h, l = lf.shape
    row = lax.broadcasted_iota(I32, (cw, cw), 0)
    col = lax.broadcasted_iota(I32, (cw, cw), 1)
    tri = jnp.where(row <= col, 1.0, 0.0).astype(BF16)
    carry = jnp.zeros((h, 1), F32)
    for c in range(l // cw):
        hi, mid, lo = _split3(lf[:, c * cw:(c + 1) * cw])
        cs = (jnp.dot(hi, tri, preferred_element_type=F32)
              + jnp.dot(mid, tri, preferred_element_type=F32)
              + jnp.dot(lo, tri, preferred_element_type=F32)) + carry
        o_ref[0, :, c * cw:(c + 1) * cw] = cs * LOG2E
        carry = cs[:, cw - 1:cw]


def _cumgate(f_t, b_f):
    b, h, l = f_t.shape
    cw = min(256, l)
    return pl.pallas_call(
        functools.partial(_cumgate_kernel, cw=cw),
        grid=(b,),
        in_specs=[
            pl.BlockSpec((1, h, l), lambda i: (i, 0, 0)),
            pl.BlockSpec((h, 1), lambda i: (0, 0)),
        ],
        out_specs=pl.BlockSpec((1, h, l), lambda i: (i, 0, 0)),
        out_shape=jax.ShapeDtypeStruct((b, h, l), F32),
        compiler_params=_params("arbitrary"),
        name="cumgate",
    )(f_t, b_f.reshape(h, 1))


def _norm_rows(xh, g, extra_scale=1.0):
    ms = jnp.mean(xh * xh, axis=0, keepdims=True)
    return xh * lax.rsqrt(ms + EPS) * (g * extra_scale)


def _kaug_prep_kernel(*refs, n_heads, mode, tk):
    if mode == "cum":
        x_ref, g_ref, c_ref, o_ref = refs
    else:
        x_ref, g_ref, o_ref = refs
    j = pl.program_id(1)
    g = g_ref[...]
    r8 = lax.broadcasted_iota(I32, (AUG_ROWS, tk), 0)
    pad = jnp.zeros((LANES - HEAD_DIM - AUG_ROWS, tk), F32)
    if mode == "pos":
        kpos = lax.broadcasted_iota(I32, (AUG_ROWS, tk), 1) + j * tk
        a = (kpos // POS_SPLIT).astype(F32)
        bb = (kpos % POS_SPLIT).astype(F32)
        aug_pos = jnp.where(r8 < 3, a, jnp.where(r8 < 6, bb, 0.0))
    for hp in range((n_heads + 1) // 2):
        xt = x_ref[0, :, hp * LANES:(hp + 1) * LANES].T
        for hh in range(min(2, n_heads - 2 * hp)):
            h = 2 * hp + hh
            kn = _norm_rows(xt[hh * HEAD_DIM:(hh + 1) * HEAD_DIM], g)
            if mode == "cum":
                hi, mid, lo = _split3(c_ref[0, h, 0])
                aug = jnp.where(r8 == 0, hi.astype(F32),
                                jnp.where(r8 == 1, mid.astype(F32), jnp.where(r8 == 2, lo.astype(F32), 0.0)))
            elif mode == "pos":
                aug = aug_pos
            else:
                aug = jnp.zeros((AUG_ROWS, tk), F32)
            t = jnp.concatenate([kn, aug, pad], axis=0)
            o_ref[0, h, 0] = t.T.astype(o_ref.dtype)


def _kaug_prep(proj, g, col0, n_heads, tk, mode, cum=None):
    b, l, _ = proj.shape
    w = max(LANES, n_heads * HEAD_DIM)
    assert col0 % w == 0
    nk = l // tk
    in_specs = [
        pl.BlockSpec((1, tk, w), lambda i, j: (i, j, col0 // w)),
        pl.BlockSpec((HEAD_DIM, 1), lambda i, j: (0, 0)),
    ]
    args = [proj, g.reshape(HEAD_DIM, 1)]
    if mode == "cum":
        in_specs.append(pl.BlockSpec((1, n_heads, 1, 1, tk), lambda i, j: (i, 0, j, 0, 0)))
        args.append(cum)
    return pl.pallas_call(
        functools.partial(_kaug_prep_kernel, n_heads=n_heads, mode=mode, tk=tk),
        grid=(b, nk),
        in_specs=in_specs,
        out_specs=pl.BlockSpec((1, n_heads, 1, tk, LANES), lambda i, j: (i, 0, j, 0, 0)),
        out_shape=jax.ShapeDtypeStruct((b, n_heads, nk, tk, LANES), BF16),
        compiler_params=_params("arbitrary", "arbitrary"),
        name="kaug_prep",
    )(*args)


def _vat_prep_kernel(x_ref, o_ref, *, n_heads, tk):
    r16 = lax.broadcasted_iota(I32, (VA_ROWS - HEAD_DIM, tk), 0)
    ones_rows = jnp.where(r16 == 0, 1.0, 0.0).astype(o_ref.dtype)
    for hp in range(n_heads // 2):
        xt = x_ref[0, :, hp * LANES:(hp + 1) * LANES].T
        for hh in range(2):
            h = 2 * hp + hh
            o_ref[0, h, 0, :HEAD_DIM] = xt[hh * HEAD_DIM:(hh + 1) * HEAD_DIM].astype(o_ref.dtype)
            o_ref[0, h, 0, HEAD_DIM:] = ones_rows


def _vat_prep(proj, col0, n_heads, tk):
    b, l, _ = proj.shape
    w = n_heads * HEAD_DIM
    assert col0 % w == 0 and n_heads % 2 == 0
    nk = l // tk
    return pl.pallas_call(
        functools.partial(_vat_prep_kernel, n_heads=n_heads, tk=tk),
        grid=(b, nk),
        in_specs=[pl.BlockSpec((1, tk, w), lambda i, j: (i, j, col0 // w))],
        out_specs=pl.BlockSpec((1, n_heads, 1, VA_ROWS, tk), lambda i, j: (i, 0, j, 0, 0)),
        out_shape=jax.ShapeDtypeStruct((b, n_heads, nk, VA_ROWS, tk), BF16),
        compiler_params=_params("arbitrary", "arbitrary"),
        name="vat_prep",
    )(proj)


def _col_reduce(x, op):
    rows, n = x.shape
    ways = 8 if rows % (8 * SUBLANES) == 0 else 1
    x = op(x.reshape(ways, rows // (ways * SUBLANES), SUBLANES, n), axis=1)
    return op(op(x, axis=0), axis=0, keepdims=True)


def _softmax_step_t(s, m_ref, acc_ref, vat, idx):
    m_prev = m_ref[idx]
    m_new = jnp.maximum(m_prev, _col_reduce(s, jnp.max))
    p = jnp.exp2(s - m_new).astype(BF16)
    alpha = jnp.exp2(m_prev - m_new)
    acc_ref[idx] = alpha * acc_ref[idx] + jnp.dot(vat, p, preferred_element_type=F32)
    m_ref[idx] = m_new


def _pipelined_chunks(nch, qk, process):
    pairs = (nch - 1) // 2
    rest = nch - 2 * pairs
    qk(0, 0)

    def body(jj, carry):
        j = 2 * jj
        qk(j + 1, 1)
        process(j, 0, False)
        qk(j + 2, 0)
        process(j + 1, 1, False)
        return carry

    lax.fori_loop(0, pairs, body, 0)

    @pl.when(rest == 1)
    def _():
        process(2 * pairs, 0, True)

    @pl.when(rest == 2)
    def _():
        qk(2 * pairs + 1, 1)
        process(2 * pairs, 0, False)
        process(2 * pairs + 1, 1, True)


def _query_weights(q_t, g, scale, aug):
    qn = _norm_rows(q_t, g, scale * LOG2E)
    pad = jnp.zeros((LANES - HEAD_DIM - AUG_ROWS, q_t.shape[1]), F32)
    return jnp.concatenate([qn, aug, pad], axis=0).astype(BF16)


def _fox_kernel(q_ref, ka_ref, vat_ref, gq_ref, o_ref, s_ref, m_ref, acc_ref, *, tq, tk, hb, scale):
    qi = pl.program_id(2)
    gq = gq_ref[...]
    r8 = lax.broadcasted_iota(I32, (AUG_ROWS, tq), 0)
    aug = jnp.where(r8 < 3, -1.0, 0.0)
    q_t = q_ref[0].T
    ws = [_query_weights(q_t[h * HEAD_DIM:(h + 1) * HEAD_DIM], gq, scale, aug) for h in range(hb)]
    m_ref[...] = jnp.full(m_ref.shape, -jnp.inf, F32)
    acc_ref[...] = jnp.zeros(acc_ref.shape, F32)
    nch = ((qi + 1) * tq + tk - 1) // tk

    def qk(j, slot):
        for h in range(hb):
            s_ref[slot, h] = jnp.dot(ka_ref[0, h, j], ws[h], preferred_element_type=F32)

    def process(j, slot, last):
        for h in range(hb):
            s = s_ref[slot, h]
            if last:
                k_pos = lax.broadcasted_iota(I32, (tk, tq), 0) + j * tk
                q_pos = lax.broadcasted_iota(I32, (tk, tq), 1) + qi * tq
                s = jnp.where(k_pos <= q_pos, s, -jnp.inf)
            _softmax_step_t(s, m_ref, acc_ref, vat_ref[0, h, j], h)

    _pipelined_chunks(nch, qk, process)
    outs = []
    for h in range(hb):
        acc = acc_ref[h]
        outs.append(acc[:HEAD_DIM] / acc[HEAD_DIM:HEAD_DIM + 1])
    o_ref[0] = jnp.concatenate(outs, axis=0).T.astype(o_ref.dtype)


def _fox_attention(proj, ka, vat, g_q, tq, tk, hb=4):
    b, l, _ = proj.shape
    h, nk = ka.shape[1], ka.shape[2]
    dh = HEAD_DIM
    hb = math.gcd(hb, h)
    return pl.pallas_call(
        functools.partial(_fox_kernel, tq=tq, tk=tk, hb=hb, scale=dh ** -0.5),
        grid=(b, h // hb, l // tq),
        in_specs=[
            pl.BlockSpec((1, tq, hb * dh), lambda i, j, k: (i, k, j)),
            _resident((1, hb, nk, tk, LANES), lambda i, j, k: (i, j, 0, 0, 0)),
            _resident((1, hb, nk, VA_ROWS, tk), lambda i, j, k: (i, j, 0, 0, 0)),
            pl.BlockSpec((dh, 1), lambda i, j, k: (0, 0)),
        ],
        out_specs=pl.BlockSpec((1, tq, hb * dh), lambda i, j, k: (i, k, j)),
        out_shape=jax.ShapeDtypeStruct((b, l, h * dh), BF16),
        scratch_shapes=[
            pltpu.VMEM((2, hb, tk, tq), F32),
            pltpu.VMEM((hb, 1, tq), F32),
            pltpu.VMEM((hb, VA_ROWS, tq), F32),
        ],
        compiler_params=_params("arbitrary", "arbitrary", "arbitrary"),
        name="fox_attn",
    )(proj, ka, vat, g_q.reshape(dh, 1))


def _ffn_kernel(x_ref, o_ref, wo_ref, gt1_ref, g_ref, sh_ref, sc_ref, gt2_ref,
                wa_ref, wg_ref, cw_ref, cb_ref, wd_ref, out_ref, abuf_ref, *, tm):
    i = pl.program_id(1)
    y = jnp.dot(o_ref[0], wo_ref[...], preferred_element_type=F32)
    x1 = x_ref[0] + gt1_ref[0] * y
    h = _modulated(x1, g_ref[...], sh_ref[0], sc_ref[0]).astype(BF16)
    a = jnp.dot(h, wa_ref[...], preferred_element_type=F32)
    gate = jnp.dot(h, wg_ref[...], preferred_element_type=F32)

    @pl.when(i == 0)
    def _():
        abuf_ref[0:8, :] = jnp.zeros((8, abuf_ref.shape[1]), F32)

    @pl.when(i > 0)
    def _():
        abuf_ref[0:8, :] = abuf_ref[tm:tm + 8, :]

    abuf_ref[8:8 + tm, :] = a
    a1 = abuf_ref[7:7 + tm, :]
    a2 = abuf_ref[6:6 + tm, :]
    cw = cw_ref[...]
    ac = cw[0:1] * a2 + cw[1:2] * a1 + cw[2:3] * a + cb_ref[...]
    act = (ac * jax.nn.sigmoid(ac) * gate).astype(BF16)
    out_ref[0] = x1 + gt2_ref[0] * jnp.dot(act, wd_ref[...], preferred_element_type=F32)


def _post_attention_ffn(x, o, wo, gt1, g, sh, sc, gt2, wa, wg, cw, cb, wd, tm=256):
    b, l, d = x.shape
    f = wa.shape[1]
    tm = min(tm, l)
    row = lambda i, j: (i, j, 0)
    per_batch = lambda i, j: (i, 0, 0)
    const = lambda i, j: (0, 0)
    return pl.pallas_call(
        functools.partial(_ffn_kernel, tm=tm),
        grid=(b, l // tm),
        in_specs=[
            pl.BlockSpec((1, tm, d), row),
            pl.BlockSpec((1, tm, d), row),
            _resident((d, d), const),
            pl.BlockSpec((1, 1, d), per_batch),
            pl.BlockSpec((1, d), const),
            pl.BlockSpec((1, 1, d), per_batch),
            pl.BlockSpec((1, 1, d), per_batch),
            pl.BlockSpec((1, 1, d), per_batch),
            _resident((d, f), const),
            _resident((d, f), const),
            pl.BlockSpec((3, f), const),
            pl.BlockSpec((1, f), const),
            _resident((f, d), const),
        ],
        out_specs=pl.BlockSpec((1, tm, d), row),
        out_shape=jax.ShapeDtypeStruct((b, l, d), F32),
        scratch_shapes=[pltpu.VMEM((tm + 8, f), F32)],
        compiler_params=_params("arbitrary", "arbitrary"),
        name="post_attn_ffn",
    )(x, o, wo, gt1.reshape(b, 1, d), g.reshape(1, d), sh.reshape(b, 1, d), sc.reshape(b, 1, d),
      gt2.reshape(b, 1, d), wa, wg, cw, cb.reshape(1, f), wd)


def _key_to_float(key):
    bits = jnp.where(key < 0, jnp.int32(-2 ** 31) - key, key)
    return lax.bitcast_convert_type(bits, F32)


def _bf16_pieces(x):
    out, r = [], float(x)
    for _ in range(3):
        p = float(np.asarray(r, dtype=ml_dtypes.bfloat16).astype(np.float32))
        out.append(p)
        r -= p
    return tuple(out)


def _dsa_kernel(q_ref, qi_ref, wi_ref, ka_ref, vat_ref, ki_ref, gq_ref, o_ref,
                s_ref, p_ref, m_ref, acc_ref, *, tq, tk, top_k, scale, slope_pieces):
    qt = pl.program_id(1)
    n_groups = ka_ref.shape[1]
    group = q_ref.shape[2] // HEAD_DIM // n_groups
    nch = ((qt + 1) * tq + tk - 1) // tk
    k_off = lax.broadcasted_iota(I32, (tk, tq), 0)
    q_pos = lax.broadcasted_iota(I32, (tk, tq), 1) + qt * tq
    kf = float(top_k)

    qi_t = qi_ref[0].T
    zpad = jnp.zeros((LANES - IDX_DIM, tq), F32)
    wqs = [jnp.concatenate([qi_t[hh * IDX_DIM:(hh + 1) * IDX_DIM], zpad], axis=0).astype(BF16)
           for hh in range(IDX_HEADS)]
    w_t = wi_ref[0].T[IDX_DIM:IDX_DIM + IDX_HEADS] * (IDX_HEADS ** -0.5 * IDX_DIM ** -0.5)

    def score_chunk(j, carry):
        kc = ki_ref[0, 0, j]
        sc = jnp.zeros((tk, tq), F32)
        for hh in range(IDX_HEADS):
            rel = jnp.maximum(jnp.dot(kc, wqs[hh], preferred_element_type=F32), 0.0)
            sc = sc + rel * w_t[hh:hh + 1]
        s_ref[j] = jnp.where(k_off + j * tk <= q_pos, sc, -jnp.inf)
        return carry

    lax.fori_loop(0, nch, score_chunk, 0)

    def count(pred):
        ways = 8 if tk % (8 * SUBLANES) == 0 else 1

        def body(j, c):
            x = jnp.where(pred(s_ref[j], j), 1.0, 0.0)
            return c + x.reshape(ways, tk // (ways * SUBLANES), SUBLANES, tq).sum(axis=1)
        c = lax.fori_loop(0, nch, body, jnp.zeros((ways, SUBLANES, tq), F32))
        return jnp.sum(jnp.sum(c, axis=0), axis=0, keepdims=True)

    def search(_):
        def cond(st):
            return st[3] > 0.0

        def step(st):
            lo, hi, clo, _ = st
            active = (clo != kf) & (hi > lo + 1)
            mid = (lo >> 1) + (hi >> 1) + (lo & hi & 1)
            cmid = _key_to_float(mid)
            cnt = count(lambda s, j: s >= cmid)
            ge = cnt >= kf
            take_lo = active & ge
            take_hi = active & jnp.logical_not(ge)
            lo = jnp.where(take_lo, mid, lo)
            clo = jnp.where(take_lo, cnt, clo)
            hi = jnp.where(take_hi, mid, hi)
            still = (clo != kf) & (hi > lo + 1)
            return lo, hi, clo, jnp.max(jnp.where(still, 1.0, 0.0))

        st0 = (jnp.full((1, tq), KEY_NEG_INF, I32), jnp.full((1, tq), KEY_POS_INF + 1, I32),
               jnp.full((1, tq), 1.0, F32) * (nch * tk).astype(F32), jnp.float32(1.0))
        lo, _, clo, _ = lax.while_loop(cond, step, st0)
        return _key_to_float(lo), clo

    def no_search(_):
        return jnp.full((1, tq), -jnp.inf, F32), jnp.full((1, tq), kf, F32)

    tau, n_ge = lax.cond((qt + 1) * tq > top_k, search, no_search, 0)
    tau_sel = jnp.maximum(tau, F32_LOWEST)

    tie = (n_ge > kf) & (tau > -jnp.inf)
    any_tie = jnp.max(jnp.where(tie, 1.0, 0.0)) > 0.0

    def mask_plain(_):
        def body(j, carry):
            s_ref[j] = jnp.where(s_ref[j] >= tau_sel, 0.0, NEG_BIG)
            return carry
        lax.fori_loop(0, nch, body, 0)
        return 0

    def mask_ties(_):
        need = kf - count(lambda s, j: s > tau)

        def step(_, lohi):
            lo, hi = lohi
            mid = (lo + hi) >> 1
            ge = count(lambda s, j: (s == tau) & (k_off + j * tk <= mid)) >= need
            return jnp.where(ge, lo, mid), jnp.where(ge, mid, hi)

        steps = max(1, int(math.ceil(math.log2(s_ref.shape[0] * tk + 1))))
        lo0 = jnp.full((1, tq), -1, I32)
        hi0 = jnp.full((1, tq), 1, I32) * (nch * tk - 1)
        _, hi = lax.fori_loop(0, steps, step, (lo0, hi0))
        jstar = jnp.where(tie, hi, jnp.int32(2 ** 30))

        def body(j, carry):
            s = s_ref[j]
            eq_bias = jnp.where(k_off + j * tk <= jstar, 0.0, NEG_BIG)
            s_ref[j] = jnp.where(s > tau_sel, 0.0, jnp.where(s == tau_sel, eq_bias, NEG_BIG))
            return carry
        lax.fori_loop(0, nch, body, 0)
        return 0

    lax.cond(any_tie, mask_ties, mask_plain, 0)

    gq = gq_ref[...]
    q_t = q_ref[0].T
    r8 = lax.broadcasted_iota(I32, (AUG_ROWS, tq), 0)
    for g in range(n_groups):
        ws = []
        for r in range(group):
            hq = g * group + r
            hi, mid, lo = slope_pieces[hq]
            rows = (POS_SPLIT * hi, POS_SPLIT * mid, POS_SPLIT * lo, hi, mid, lo)
            aug = jnp.zeros((AUG_ROWS, tq), F32)
            for i, val in enumerate(rows):
                aug = jnp.where(r8 == i, val, aug)
            ws.append(_query_weights(q_t[hq * HEAD_DIM:(hq + 1) * HEAD_DIM], gq, scale, aug))
        wg = jnp.concatenate(ws, axis=1)
        m_ref[...] = jnp.full(m_ref.shape, -jnp.inf, F32)
        acc_ref[...] = jnp.zeros(acc_ref.shape, F32)

        def qk(j, slot, g=g, wg=wg):
            p_ref[slot] = jnp.dot(ka_ref[0, g, j], wg, preferred_element_type=F32)

        def process(j, slot, last, g=g):
            s = p_ref[slot] + jnp.tile(s_ref[j], (1, group))
            _softmax_step_t(s, m_ref, acc_ref, vat_ref[0, g, j], 0)

        _pipelined_chunks(nch, qk, process)
        acc = acc_ref[0]
        out = acc[:HEAD_DIM] / acc[HEAD_DIM:HEAD_DIM + 1]
        out = jnp.concatenate([out[:, r * tq:(r + 1) * tq] for r in range(group)], axis=0)
        gw = group * HEAD_DIM
        o_ref[0, :, g * gw:(g + 1) * gw] = out.T.astype(o_ref.dtype)


def _dsa_attention(proj, ka, vat, ki, g_q, n_q_heads, qi_col0, wi_col0, tq, tk, top_k):
    b, l, _ = proj.shape
    dh = HEAD_DIM
    n_groups, nk = ka.shape[1], ka.shape[2]
    qw = n_q_heads * dh
    iw = IDX_HEADS * IDX_DIM
    assert qi_col0 % iw == 0 and wi_col0 % LANES == IDX_DIM
    slope_pieces = tuple(_bf16_pieces(2.0 ** (-8.0 * (i + 1) / n_q_heads) * LOG2E) for i in range(n_q_heads))
    cols = (n_q_heads // n_groups) * tq
    return pl.pallas_call(
        functools.partial(_dsa_kernel, tq=tq, tk=tk, top_k=top_k, scale=dh ** -0.5, slope_pieces=slope_pieces),
        grid=(b, l // tq),
        in_specs=[
            pl.BlockSpec((1, tq, qw), lambda i, j: (i, j, 0)),
            pl.BlockSpec((1, tq, iw), lambda i, j: (i, j, qi_col0 // iw)),
            pl.BlockSpec((1, tq, LANES), lambda i, j: (i, j, wi_col0 // LANES)),
            _resident((1, n_groups, nk, tk, LANES), lambda i, j: (i, 0, 0, 0, 0)),
            _resident((1, n_groups, nk, VA_ROWS, tk), lambda i, j: (i, 0, 0, 0, 0)),
            _resident((1, 1, nk, tk, LANES), lambda i, j: (i, 0, 0, 0, 0)),
            pl.BlockSpec((dh, 1), lambda i, j: (0, 0)),
        ],
        out_specs=pl.BlockSpec((1, tq, qw), lambda i, j: (i, j, 0)),
        out_shape=jax.ShapeDtypeStruct((b, l, qw), BF16),
        scratch_shapes=[
            pltpu.VMEM((nk, tk, tq), F32),
            pltpu.VMEM((2, tk, cols), F32),
            pltpu.VMEM((1, 1, cols), F32),
            pltpu.VMEM((1, VA_ROWS, cols), F32),
        ],
        compiler_params=_params("arbitrary", "arbitrary"),
        name="dsa_attn",
    )(proj, proj, proj, ka, vat, ki, g_q.reshape(dh, 1))


def _pad_cols(w, mult=LANES):
    n = w.shape[1]
    return jnp.pad(w, ((0, 0), (0, _round_up(n, mult) - n)))


def kernel(x, c, w_ada, b_ada, g_norm_mix, g_norm_ffn, fox_w_in, fox_b_f, fox_g_q, fox_g_k, fox_w_out,
           dsa_w_in, dsa_g_q, dsa_g_k, dsa_g_kidx, dsa_w_out, ffn_w_up, ffn_conv_w, ffn_conv_b, ffn_w_down):
    b, l, d = x.shape
    dh = HEAD_DIM
    n_heads = d // dh
    d_ff = ffn_w_down.shape[1]
    top_k = min(TOPK_MAX, l // 4)
    tq = min(Q_TILE, l)
    tk = min(K_TILE, l)

    mod = _adaln(c, w_ada, b_ada)

    def ffn(i, xx, o, w_out):
        sh2, sc2, gt2 = mod[i, :, 3 * d:4 * d], mod[i, :, 4 * d:5 * d], mod[i, :, 5 * d:6 * d]
        gt1 = mod[i, :, 2 * d:3 * d]
        wup = ffn_w_up[i].astype(BF16)
        return _post_attention_ffn(xx, o, w_out.astype(BF16), gt1, g_norm_ffn[i], sh2, sc2, gt2,
                                   wup[:, :d_ff], wup[:, d_ff:], ffn_conv_w[i], ffn_conv_b[i],
                                   ffn_w_down[i].astype(BF16))

    proj = _inproj(x, g_norm_mix[0], mod[0, :, 0:d], mod[0, :, d:2 * d], _pad_cols(fox_w_in[0]).astype(BF16))
    f_t = proj[..., 3 * d:3 * d + n_heads].transpose(0, 2, 1)
    cum = _cumgate(f_t, fox_b_f[0]).reshape(b, n_heads, l // tk, 1, tk)
    ka = _kaug_prep(proj, fox_g_k[0], d, n_heads, tk, "cum", cum)
    vat = _vat_prep(proj, 2 * d, n_heads, tk)
    o = _fox_attention(proj, ka, vat, fox_g_q[0], tq, tk)
    x = ffn(0, x, o, fox_w_out[0])

    g_kv = DSA_KV_HEADS
    kvw = g_kv * dh
    proj = _inproj(x, g_norm_mix[1], mod[1, :, 0:d], mod[1, :, d:2 * d], _pad_cols(dsa_w_in[0]).astype(BF16))
    o0, o1, o2 = d, d + kvw, d + 2 * kvw
    o3 = o2 + IDX_HEADS * IDX_DIM
    o4 = o3 + IDX_DIM
    ka = _kaug_prep(proj, dsa_g_k[0], o0, g_kv, tk, "pos")
    vat = _vat_prep(proj, o1, g_kv, tk)
    ki = _kaug_prep(proj, dsa_g_kidx[0], o3, 1, tk, "none")
    o = _dsa_attention(proj, ka, vat, ki, dsa_g_q[0], n_heads, o2, o4, tq, tk, top_k)
    x = ffn(1, x, o, dsa_w_out[0])
    return x
```

```python
import functools
import math

import jax
import jax.numpy as jnp
import ml_dtypes
import numpy as np
from jax import lax
from jax.experimental import pallas as pl
from jax.experimental.pallas import tpu as pltpu

F32 = jnp.float32
BF16 = jnp.bfloat16
I32 = jnp.int32

HEAD_DIM = 64
DSA_KV_HEADS = 4
IDX_HEADS = 8
IDX_DIM = 64
TOPK_MAX = 256
EPS = 1e-6
LOG2E = 1.4426950408889634
LANES = 128
SUBLANES = 8
VMEM_LIMIT = 56 * 1024 * 1024
NEG_BIG = -1e30
F32_LOWEST = -3.4028234663852886e38
KEY_NEG_INF = -0x7F800000
KEY_POS_INF = 0x7F800000
Q_TILE = 256
K_TILE = 512
AUG_ROWS = SUBLANES
VA_ROWS = 80
POS_SPLIT = 256


def _params(*sem):
    return pltpu.CompilerParams(dimension_semantics=sem, vmem_limit_bytes=VMEM_LIMIT)


def _resident(shape, index_map):
    return pl.BlockSpec(shape, index_map, pipeline_mode=pl.Buffered(1))


def _round_up(n, m):
    return (n + m - 1) // m * m


def _adaln_kernel(c_ref, w_ref, b_ref, o_ref):
    c = c_ref[...]
    ca = c * jax.nn.sigmoid(c)
    o_ref[0] = jnp.dot(ca, w_ref[0], preferred_element_type=F32) + b_ref[0]


def _adaln(c, w_ada, b_ada):
    depth, d, n = w_ada.shape
    b = c.shape[0]
    rows = _round_up(b, 8)
    c_pad = jnp.pad(c, ((0, rows - b), (0, 0)))
    tn = 1536
    out = pl.pallas_call(
        _adaln_kernel,
        grid=(depth, n // tn),
        in_specs=[
            pl.BlockSpec((rows, d), lambda i, j: (0, 0)),
            pl.BlockSpec((1, d, tn), lambda i, j: (i, 0, j)),
            pl.BlockSpec((1, 1, tn), lambda i, j: (i, 0, j)),
        ],
        out_specs=pl.BlockSpec((1, rows, tn), lambda i, j: (i, 0, j)),
        out_shape=jax.ShapeDtypeStruct((depth, rows, n), F32),
        compiler_params=_params("arbitrary", "arbitrary"),
        name="adaln",
    )(c_pad, w_ada, b_ada.reshape(depth, 1, n))
    return out[:, :b]


def _modulated(x, g, sh, sc):
    ms = jnp.mean(x * x, axis=-1, keepdims=True)
    return x * lax.rsqrt(ms + EPS) * (g * (1.0 + sc)) + sh


def _inproj_kernel(x_ref, g_ref, sh_ref, sc_ref, w_ref, o_ref):
    h = _modulated(x_ref[0], g_ref[...], sh_ref[0], sc_ref[0])
    o_ref[0] = jnp.dot(h.astype(BF16), w_ref[...], preferred_element_type=F32)


def _inproj(x, g, sh, sc, w_bf16, tm=512):
    b, l, d = x.shape
    n = w_bf16.shape[1]
    tm = min(tm, l)
    return pl.pallas_call(
        _inproj_kernel,
        grid=(b, l // tm),
        in_specs=[
            pl.BlockSpec((1, tm, d), lambda i, j: (i, j, 0)),
            pl.BlockSpec((1, d), lambda i, j: (0, 0)),
            pl.BlockSpec((1, 1, d), lambda i, j: (i, 0, 0)),
            pl.BlockSpec((1, 1, d), lambda i, j: (i, 0, 0)),
            _resident((d, n), lambda i, j: (0, 0)),
        ],
        out_specs=pl.BlockSpec((1, tm, n), lambda i, j: (i, j, 0)),
        out_shape=jax.ShapeDtypeStruct((b, l, n), F32),
        compiler_params=_params("arbitrary", "arbitrary"),
        name="inproj",
    )(x, g.reshape(1, d), sh.reshape(b, 1, d), sc.reshape(b, 1, d), w_bf16)


def _split3(x):
    hi = x.astype(BF16)
    r = x - hi.astype(F32)
    mid = r.astype(BF16)
    lo = (r - mid.astype(F32)).astype(BF16)
    return hi, mid, lo


def _cumgate_kernel(f_ref, b_ref, o_ref, *, cw):
    x = f_ref[0] + b_ref[...]
    lf = jnp.minimum(x, 0.0) - jnp.log1p(jnp.exp(-jnp.abs(x)))
    h, l = lf.shape
    row = lax.broadcasted_iota(I32, (cw, cw), 0)
    col = lax.broadcasted_iota(I32, (cw, cw), 1)
    tri = jnp.where(row <= col, 1.0, 0.0).astype(BF16)
    carry = jnp.zeros((h, 1), F32)
    for c in range(l // cw):
        hi, mid, lo = _split3(lf[:, c * cw:(c + 1) * cw])
        cs = (jnp.dot(hi, tri, preferred_element_type=F32)
              + jnp.dot(mid, tri, preferred_element_type=F32)
              + jnp.dot(lo, tri, preferred_element_type=F32)) + carry
        o_ref[0, :, c * cw:(c + 1) * cw] = cs * LOG2E
        carry = cs[:, cw - 1:cw]


def _cumgate(f_t, b_f):
    b, h, l = f_t.shape
    cw = min(256, l)
    return pl.pallas_call(
        functools.partial(_cumgate_kernel, cw=cw),
        grid=(b,),
        in_specs=[
            pl.BlockSpec((1, h, l), lambda i: (i, 0, 0)),
            pl.BlockSpec((h, 1), lambda i: (0, 0)),
        ],
        out_specs=pl.BlockSpec((1, h, l), lambda i: (i, 0, 0)),
        out_shape=jax.ShapeDtypeStruct((b, h, l), F32),
        compiler_params=_params("arbitrary"),
        name="cumgate",
    )(f_t, b_f.reshape(h, 1))


def _norm_rows(xh, g, extra_scale=1.0):
    ms = jnp.mean(xh * xh, axis=0, keepdims=True)
    return xh * lax.rsqrt(ms + EPS) * (g * extra_scale)


def _kaug_prep_kernel(*refs, n_heads, mode, tk):
    if mode == "cum":
        x_ref, g_ref, c_ref, o_ref = refs
    else:
        x_ref, g_ref, o_ref = refs
    j = pl.program_id(1)
    g = g_ref[...]
    r8 = lax.broadcasted_iota(I32, (AUG_ROWS, tk), 0)
    pad = jnp.zeros((LANES - HEAD_DIM - AUG_ROWS, tk), F32)
    if mode == "pos":
        kpos = lax.broadcasted_iota(I32, (AUG_ROWS, tk), 1) + j * tk
        a = (kpos // POS_SPLIT).astype(F32)
        bb = (kpos % POS_SPLIT).astype(F32)
        aug_pos = jnp.where(r8 < 3, a, jnp.where(r8 < 6, bb, 0.0))
    for hp in range((n_heads + 1) // 2):
        xt = x_ref[0, :, hp * LANES:(hp + 1) * LANES].T
        for hh in range(min(2, n_heads - 2 * hp)):
            h = 2 * hp + hh
            kn = _norm_rows(xt[hh * HEAD_DIM:(hh + 1) * HEAD_DIM], g)
            if mode == "cum":
                hi, mid, lo = _split3(c_ref[0, h, 0])
                aug = jnp.where(r8 == 0, hi.astype(F32),
                                jnp.where(r8 == 1, mid.astype(F32), jnp.where(r8 == 2, lo.astype(F32), 0.0)))
            elif mode == "pos":
                aug = aug_pos
            else:
                aug = jnp.zeros((AUG_ROWS, tk), F32)
            t = jnp.concatenate([kn, aug, pad], axis=0)
            o_ref[0, h, 0] = t.T.astype(o_ref.dtype)


def _kaug_prep(proj, g, col0, n_heads, tk, mode, cum=None):
    b, l, _ = proj.shape
    w = max(LANES, n_heads * HEAD_DIM)
    assert col0 % w == 0
    nk = l // tk
    in_specs = [
        pl.BlockSpec((1, tk, w), lambda i, j: (i, j, col0 // w)),
        pl.BlockSpec((HEAD_DIM, 1), lambda i, j: (0, 0)),
    ]
    args = [proj, g.reshape(HEAD_DIM, 1)]
    if mode == "cum":
        in_specs.append(pl.BlockSpec((1, n_heads, 1, 1, tk), lambda i, j: (i, 0, j, 0, 0)))
        args.append(cum)
    return pl.pallas_call(
        functools.partial(_kaug_prep_kernel, n_heads=n_heads, mode=mode, tk=tk),
        grid=(b, nk),
        in_specs=in_specs,
        out_specs=pl.BlockSpec((1, n_heads, 1, tk, LANES), lambda i, j: (i, 0, j, 0, 0)),
        out_shape=jax.ShapeDtypeStruct((b, n_heads, nk, tk, LANES), BF16),
        compiler_params=_params("arbitrary", "arbitrary"),
        name="kaug_prep",
    )(*args)


def _vat_prep_kernel(x_ref, o_ref, *, n_heads, tk):
    r16 = lax.broadcasted_iota(I32, (VA_ROWS - HEAD_DIM, tk), 0)
    ones_rows = jnp.where(r16 == 0, 1.0, 0.0).astype(o_ref.dtype)
    for hp in range(n_heads // 2):
        xt = x_ref[0, :, hp * LANES:(hp + 1) * LANES].T
        for hh in range(2):
            h = 2 * hp + hh
            o_ref[0, h, 0, :HEAD_DIM] = xt[hh * HEAD_DIM:(hh + 1) * HEAD_DIM].astype(o_ref.dtype)
            o_ref[0, h, 0, HEAD_DIM:] = ones_rows


def _vat_prep(proj, col0, n_heads, tk):
    b, l, _ = proj.shape
    w = n_heads * HEAD_DIM
    assert col0 % w == 0 and n_heads % 2 == 0
    nk = l // tk
    return pl.pallas_call(
        functools.partial(_vat_prep_kernel, n_heads=n_heads, tk=tk),
        grid=(b, nk),
        in_specs=[pl.BlockSpec((1, tk, w), lambda i, j: (i, j, col0 // w))],
        out_specs=pl.BlockSpec((1, n_heads, 1, VA_ROWS, tk), lambda i, j: (i, 0, j, 0, 0)),
        out_shape=jax.ShapeDtypeStruct((b, n_heads, nk, VA_ROWS, tk), BF16),
        compiler_params=_params("arbitrary", "arbitrary"),
        name="vat_prep",
    )(proj)


def _col_reduce(x, op):
    rows, n = x.shape
    ways = 8 if rows % (8 * SUBLANES) == 0 else 1
    x = op(x.reshape(ways, rows // (ways * SUBLANES), SUBLANES, n), axis=1)
    return op(op(x, axis=0), axis=0, keepdims=True)


def _softmax_step_t(s, m_ref, acc_ref, vat, idx):
    m_prev = m_ref[idx]
    m_new = jnp.maximum(m_prev, _col_reduce(s, jnp.max))
    p = jnp.exp2(s - m_new).astype(BF16)
    alpha = jnp.exp2(m_prev - m_new)
    acc_ref[idx] = alpha * acc_ref[idx] + jnp.dot(vat, p, preferred_element_type=F32)
    m_ref[idx] = m_new


def _pipelined_chunks(nch, qk, process):
    pairs = (nch - 1) // 2
    rest = nch - 2 * pairs
    qk(0, 0)

    def body(jj, carry):
        j = 2 * jj
        qk(j + 1, 1)
        process(j, 0, False)
        qk(j + 2, 0)
        process(j + 1, 1, False)
        return carry

    lax.fori_loop(0, pairs, body, 0)

    @pl.when(rest == 1)
    def _():
        process(2 * pairs, 0, True)

    @pl.when(rest == 2)
    def _():
        qk(2 * pairs + 1, 1)
        process(2 * pairs, 0, False)
        process(2 * pairs + 1, 1, True)


def _query_weights(q_t, g, scale, aug):
    qn = _norm_rows(q_t, g, scale * LOG2E)
    pad = jnp.zeros((LANES - HEAD_DIM - AUG_ROWS, q_t.shape[1]), F32)
    return jnp.concatenate([qn, aug, pad], axis=0).astype(BF16)


def _fox_kernel(q_ref, ka_ref, vat_ref, gq_ref, o_ref, s_ref, m_ref, acc_ref, *, tq, tk, hb, scale):
    qi = pl.program_id(2)
    gq = gq_ref[...]
    r8 = lax.broadcasted_iota(I32, (AUG_ROWS, tq), 0)
    aug = jnp.where(r8 < 3, -1.0, 0.0)
    q_t = q_ref[0].T
    ws = [_query_weights(q_t[h * HEAD_DIM:(h + 1) * HEAD_DIM], gq, scale, aug) for h in range(hb)]
    m_ref[...] = jnp.full(m_ref.shape, -jnp.inf, F32)
    acc_ref[...] = jnp.zeros(acc_ref.shape, F32)
    nch = ((qi + 1) * tq + tk - 1) // tk

    def qk(j, slot):
        for h in range(hb):
            s_ref[slot, h] = jnp.dot(ka_ref[0, h, j], ws[h], preferred_element_type=F32)

    def process(j, slot, last):
        for h in range(hb):
            s = s_ref[slot, h]
            if last:
                k_pos = lax.broadcasted_iota(I32, (tk, tq), 0) + j * tk
                q_pos = lax.broadcasted_iota(I32, (tk, tq), 1) + qi * tq
                s = jnp.where(k_pos <= q_pos, s, -jnp.inf)
            _softmax_step_t(s, m_ref, acc_ref, vat_ref[0, h, j], h)

    _pipelined_chunks(nch, qk, process)
    outs = []
    for h in range(hb):
        acc = acc_ref[h]
        outs.append(acc[:HEAD_DIM] / acc[HEAD_DIM:HEAD_DIM + 1])
    o_ref[0] = jnp.concatenate(outs, axis=0).T.astype(o_ref.dtype)


def _fox_attention(proj, ka, vat, g_q, tq, tk, hb=8):
    b, l, _ = proj.shape
    h, nk = ka.shape[1], ka.shape[2]
    dh = HEAD_DIM
    hb = math.gcd(hb, h)
    return pl.pallas_call(
        functools.partial(_fox_kernel, tq=tq, tk=tk, hb=hb, scale=dh ** -0.5),
        grid=(b, h // hb, l // tq),
        in_specs=[
            pl.BlockSpec((1, tq, hb * dh), lambda i, j, k: (i, k, j)),
            _resident((1, hb, nk, tk, LANES), lambda i, j, k: (i, j, 0, 0, 0)),
            _resident((1, hb, nk, VA_ROWS, tk), lambda i, j, k: (i, j, 0, 0, 0)),
            pl.BlockSpec((dh, 1), lambda i, j, k: (0, 0)),
        ],
        out_specs=pl.BlockSpec((1, tq, hb * dh), lambda i, j, k: (i, k, j)),
        out_shape=jax.ShapeDtypeStruct((b, l, h * dh), BF16),
        scratch_shapes=[
            pltpu.VMEM((2, hb, tk, tq), F32),
            pltpu.VMEM((hb, 1, tq), F32),
            pltpu.VMEM((hb, VA_ROWS, tq), F32),
        ],
        compiler_params=_params("arbitrary", "arbitrary", "arbitrary"),
        name="fox_attn",
    )(proj, ka, vat, g_q.reshape(dh, 1))


def _ffn_kernel(x_ref, o_ref, wo_ref, gt1_ref, g_ref, sh_ref, sc_ref, gt2_ref,
                wa_ref, wg_ref, cw_ref, cb_ref, wd_ref, out_ref, abuf_ref, *, tm, fc):
    i = pl.program_id(1)
    y = jnp.dot(o_ref[0], wo_ref[...], preferred_element_type=F32)
    x1 = x_ref[0] + gt1_ref[0] * y
    h = _modulated(x1, g_ref[...], sh_ref[0], sc_ref[0]).astype(BF16)

    @pl.when(i == 0)
    def _():
        abuf_ref[0:8, :] = jnp.zeros((8, abuf_ref.shape[1]), F32)

    @pl.when(i > 0)
    def _():
        abuf_ref[0:8, :] = abuf_ref[tm:tm + 8, :]

    f = abuf_ref.shape[1]
    y2 = None
    for c in range(f // fc):
        sl = slice(c * fc, (c + 1) * fc)
        a = jnp.dot(h, wa_ref[:, sl], preferred_element_type=F32)
        gate = jnp.dot(h, wg_ref[:, sl], preferred_element_type=F32)
        abuf_ref[8:8 + tm, sl] = a
        a1 = abuf_ref[7:7 + tm, sl]
        a2 = abuf_ref[6:6 + tm, sl]
        ac = cw_ref[0:1, sl] * a2 + cw_ref[1:2, sl] * a1 + cw_ref[2:3, sl] * a + cb_ref[:, sl]
        act = (ac * jax.nn.sigmoid(ac) * gate).astype(BF16)
        part = jnp.dot(act, wd_ref[sl, :], preferred_element_type=F32)
        y2 = part if y2 is None else y2 + part
    out_ref[0] = x1 + gt2_ref[0] * y2


def _post_attention_ffn(x, o, wo, gt1, g, sh, sc, gt2, wa, wg, cw, cb, wd, tm=512):
    b, l, d = x.shape
    f = wa.shape[1]
    tm = min(tm, l)
    row = lambda i, j: (i, j, 0)
    per_batch = lambda i, j: (i, 0, 0)
    const = lambda i, j: (0, 0)
    return pl.pallas_call(
        functools.partial(_ffn_kernel, tm=tm, fc=math.gcd(f, 256)),
        grid=(b, l // tm),
        in_specs=[
            pl.BlockSpec((1, tm, d), row),
            pl.BlockSpec((1, tm, d), row),
            _resident((d, d), const),
            pl.BlockSpec((1, 1, d), per_batch),
            pl.BlockSpec((1, d), const),
            pl.BlockSpec((1, 1, d), per_batch),
            pl.BlockSpec((1, 1, d), per_batch),
            pl.BlockSpec((1, 1, d), per_batch),
            _resident((d, f), const),
            _resident((d, f), const),
            pl.BlockSpec((3, f), const),
            pl.BlockSpec((1, f), const),
            _resident((f, d), const),
        ],
        out_specs=pl.BlockSpec((1, tm, d), row),
        out_shape=jax.ShapeDtypeStruct((b, l, d), F32),
        scratch_shapes=[pltpu.VMEM((tm + 8, f), F32)],
        compiler_params=_params("arbitrary", "arbitrary"),
        name="post_attn_ffn",
    )(x, o, wo, gt1.reshape(b, 1, d), g.reshape(1, d), sh.reshape(b, 1, d), sc.reshape(b, 1, d),
      gt2.reshape(b, 1, d), wa, wg, cw, cb.reshape(1, f), wd)


def _key_to_float(key):
    bits = jnp.where(key < 0, jnp.int32(-2 ** 31) - key, key)
    return lax.bitcast_convert_type(bits, F32)


def _bf16_pieces(x):
    out, r = [], float(x)
    for _ in range(3):
        p = float(np.asarray(r, dtype=ml_dtypes.bfloat16).astype(np.float32))
        out.append(p)
        r -= p
    return tuple(out)


def _dsa_kernel(q_ref, qi_ref, wi_ref, ka_ref, vat_ref, ki_ref, gq_ref, o_ref,
                s_ref, b_ref, p_ref, m_ref, acc_ref, *, tq, tk, top_k, scale, slope_pieces):
    qt = pl.program_id(1)
    n_groups = ka_ref.shape[1]
    group = q_ref.shape[2] // HEAD_DIM // n_groups
    nch = ((qt + 1) * tq + tk - 1) // tk
    k_off = lax.broadcasted_iota(I32, (tk, tq), 0)
    q_pos = lax.broadcasted_iota(I32, (tk, tq), 1) + qt * tq
    kf = float(top_k)

    qi_t = qi_ref[0].T
    zpad = jnp.zeros((LANES - IDX_DIM, tq), F32)
    wqs = [jnp.concatenate([qi_t[hh * IDX_DIM:(hh + 1) * IDX_DIM], zpad], axis=0).astype(BF16)
           for hh in range(IDX_HEADS)]
    w_t = wi_ref[0].T[IDX_DIM:IDX_DIM + IDX_HEADS] * (IDX_HEADS ** -0.5 * IDX_DIM ** -0.5)

    def score_chunk(j, carry):
        kc = ki_ref[0, 0, j]
        sc = jnp.zeros((tk, tq), F32)
        for hh in range(IDX_HEADS):
            rel = jnp.maximum(jnp.dot(kc, wqs[hh], preferred_element_type=F32), 0.0)
            sc = sc + rel * w_t[hh:hh + 1]
        sc = jnp.where(k_off + j * tk <= q_pos, sc, -jnp.inf)
        s_ref[j] = sc
        b_ref[j] = sc.astype(BF16)
        return carry

    lax.fori_loop(0, nch, score_chunk, 0)

    def count(pred):
        ways = 8 if tk % (8 * SUBLANES) == 0 else 1

        def body(j, c):
            x = jnp.where(pred(s_ref[j], j), 1.0, 0.0)
            return c + x.reshape(ways, tk // (ways * SUBLANES), SUBLANES, tq).sum(axis=1)
        c = lax.fori_loop(0, nch, body, jnp.zeros((ways, SUBLANES, tq), F32))
        return jnp.sum(jnp.sum(c, axis=0), axis=0, keepdims=True)

    def count_coarse(c16):
        rows = 2 * SUBLANES
        ways = 8 if tk % (8 * rows) == 0 else 1
        one, zero = jnp.ones((), BF16), jnp.zeros((), BF16)

        def body(j, c):
            x = jnp.where(b_ref[j] >= c16, one, zero).reshape(ways, tk // (ways * rows), rows, tq)
            for i in range(x.shape[1]):
                c = c + x[:, i]
            return c
        c = lax.fori_loop(0, nch, body, jnp.zeros((ways, rows, tq), BF16))
        return jnp.sum(jnp.sum(c.astype(F32), axis=0), axis=0, keepdims=True)

    def search(_):
        def coarse_step(_, lohi):
            lo, hi = lohi
            mid = (lo + hi) >> 1
            ge = count_coarse(_key_to_float(mid << 16).astype(BF16)) >= kf
            return jnp.where(ge, mid, lo), jnp.where(ge, hi, mid)

        lo16, hi16 = lax.fori_loop(0, 16, coarse_step, (jnp.full((1, tq), KEY_NEG_INF >> 16, I32),
                                                        jnp.full((1, tq), (KEY_POS_INF >> 16) + 1, I32)))

        def flag(lo, hi, clo):
            return jnp.max(jnp.where((clo != kf) & (hi > lo + 1), 1.0, 0.0))

        def cond(st):
            return st[3] > 0.0

        def step(st):
            lo, hi, clo, _ = st
            active = (clo != kf) & (hi > lo + 1)
            mid = (lo >> 1) + (hi >> 1) + (lo & hi & 1)
            cmid = _key_to_float(mid)
            cnt = count(lambda s, j: s >= cmid)
            ge = cnt >= kf
            take_lo = active & ge
            take_hi = active & jnp.logical_not(ge)
            lo = jnp.where(take_lo, mid, lo)
            clo = jnp.where(take_lo, cnt, clo)
            hi = jnp.where(take_hi, mid, hi)
            return lo, hi, clo, flag(lo, hi, clo)

        lo0 = jnp.maximum((lo16 - 1) << 16, KEY_NEG_INF)
        hi0 = hi16 << 16
        f_lo0 = _key_to_float(lo0)
        clo0 = count(lambda s, j: s >= f_lo0)
        lo, _, clo, _ = lax.while_loop(cond, step, (lo0, hi0, clo0, flag(lo0, hi0, clo0)))
        return _key_to_float(lo), clo

    def no_search(_):
        return jnp.full((1, tq), -jnp.inf, F32), jnp.full((1, tq), kf, F32)

    tau, n_ge = lax.cond((qt + 1) * tq > top_k, search, no_search, 0)
    tau_sel = jnp.maximum(tau, F32_LOWEST)

    tie = (n_ge > kf) & (tau > -jnp.inf)
    any_tie = jnp.max(jnp.where(tie, 1.0, 0.0)) > 0.0

    def mask_plain(_):
        def body(j, carry):
            s_ref[j] = jnp.where(s_ref[j] >= tau_sel, 0.0, NEG_BIG)
            return carry
        lax.fori_loop(0, nch, body, 0)
        return 0

    def mask_ties(_):
        need = kf - count(lambda s, j: s > tau)

        def step(_, lohi):
            lo, hi = lohi
            mid = (lo + hi) >> 1
            ge = count(lambda s, j: (s == tau) & (k_off + j * tk <= mid)) >= need
            return jnp.where(ge, lo, mid), jnp.where(ge, mid, hi)

        steps = max(1, int(math.ceil(math.log2(s_ref.shape[0] * tk + 1))))
        lo0 = jnp.full((1, tq), -1, I32)
        hi0 = jnp.full((1, tq), 1, I32) * (nch * tk - 1)
        _, hi = lax.fori_loop(0, steps, step, (lo0, hi0))
        jstar = jnp.where(tie, hi, jnp.int32(2 ** 30))

        def body(j, carry):
            s = s_ref[j]
            eq_bias = jnp.where(k_off + j * tk <= jstar, 0.0, NEG_BIG)
            s_ref[j] = jnp.where(s > tau_sel, 0.0, jnp.where(s == tau_sel, eq_bias, NEG_BIG))
            return carry
        lax.fori_loop(0, nch, body, 0)
        return 0

    lax.cond(any_tie, mask_ties, mask_plain, 0)

    gq = gq_ref[...]
    q_t = q_ref[0].T
    r8 = lax.broadcasted_iota(I32, (AUG_ROWS, tq), 0)
    for g in range(n_groups):
        ws = []
        for r in range(group):
            hq = g * group + r
            hi, mid, lo = slope_pieces[hq]
            rows = (POS_SPLIT * hi, POS_SPLIT * mid, POS_SPLIT * lo, hi, mid, lo)
            aug = jnp.zeros((AUG_ROWS, tq), F32)
            for i, val in enumerate(rows):
                aug = jnp.where(r8 == i, val, aug)
            ws.append(_query_weights(q_t[hq * HEAD_DIM:(hq + 1) * HEAD_DIM], gq, scale, aug))
        wg = jnp.concatenate(ws, axis=1)
        m_ref[...] = jnp.full(m_ref.shape, -jnp.inf, F32)
        acc_ref[...] = jnp.zeros(acc_ref.shape, F32)

        def qk(j, slot, g=g, wg=wg):
            p_ref[slot] = jnp.dot(ka_ref[0, g, j], wg, preferred_element_type=F32)

        def process(j, slot, last, g=g):
            s = p_ref[slot] + jnp.tile(s_ref[j], (1, group))
            _softmax_step_t(s, m_ref, acc_ref, vat_ref[0, g, j], 0)

        _pipelined_chunks(nch, qk, process)
        acc = acc_ref[0]
        out = acc[:HEAD_DIM] / acc[HEAD_DIM:HEAD_DIM + 1]
        out = jnp.concatenate([out[:, r * tq:(r + 1) * tq] for r in range(group)], axis=0)
        gw = group * HEAD_DIM
        o_ref[0, :, g * gw:(g + 1) * gw] = out.T.astype(o_ref.dtype)


def _dsa_attention(proj, ka, vat, ki, g_q, n_q_heads, qi_col0, wi_col0, tq, tk, top_k):
    b, l, _ = proj.shape
    dh = HEAD_DIM
    n_groups, nk = ka.shape[1], ka.shape[2]
    qw = n_q_heads * dh
    iw = IDX_HEADS * IDX_DIM
    assert qi_col0 % iw == 0 and wi_col0 % LANES == IDX_DIM
    slope_pieces = tuple(_bf16_pieces(2.0 ** (-8.0 * (i + 1) / n_q_heads) * LOG2E) for i in range(n_q_heads))
    cols = (n_q_heads // n_groups) * tq
    return pl.pallas_call(
        functools.partial(_dsa_kernel, tq=tq, tk=tk, top_k=top_k, scale=dh ** -0.5, slope_pieces=slope_pieces),
        grid=(b, l // tq),
        in_specs=[
            pl.BlockSpec((1, tq, qw), lambda i, j: (i, j, 0)),
            pl.BlockSpec((1, tq, iw), lambda i, j: (i, j, qi_col0 // iw)),
            pl.BlockSpec((1, tq, LANES), lambda i, j: (i, j, wi_col0 // LANES)),
            _resident((1, n_groups, nk, tk, LANES), lambda i, j: (i, 0, 0, 0, 0)),
            _resident((1, n_groups, nk, VA_ROWS, tk), lambda i, j: (i, 0, 0, 0, 0)),
            _resident((1, 1, nk, tk, LANES), lambda i, j: (i, 0, 0, 0, 0)),
            pl.BlockSpec((dh, 1), lambda i, j: (0, 0)),
        ],
        out_specs=pl.BlockSpec((1, tq, qw), lambda i, j: (i, j, 0)),
        out_shape=jax.ShapeDtypeStruct((b, l, qw), BF16),
        scratch_shapes=[
            pltpu.VMEM((nk, tk, tq), F32),
            pltpu.VMEM((nk, tk, tq), BF16),
            pltpu.VMEM((2, tk, cols), F32),
            pltpu.VMEM((1, 1, cols), F32),
            pltpu.VMEM((1, VA_ROWS, cols), F32),
        ],
        compiler_params=_params("arbitrary", "arbitrary"),
        name="dsa_attn",
    )(proj, proj, proj, ka, vat, ki, g_q.reshape(dh, 1))


def _pad_cols(w, mult=LANES):
    n = w.shape[1]
    return jnp.pad(w, ((0, 0), (0, _round_up(n, mult) - n)))


def kernel(x, c, w_ada, b_ada, g_norm_mix, g_norm_ffn, fox_w_in, fox_b_f, fox_g_q, fox_g_k, fox_w_out,
           dsa_w_in, dsa_g_q, dsa_g_k, dsa_g_kidx, dsa_w_out, ffn_w_up, ffn_conv_w, ffn_conv_b, ffn_w_down):
    b, l, d = x.shape
    dh = HEAD_DIM
    n_heads = d // dh
    d_ff = ffn_w_down.shape[1]
    top_k = min(TOPK_MAX, l // 4)
    tq = min(Q_TILE, l)
    tk = min(K_TILE, l)

    mod = _adaln(c, w_ada, b_ada)

    def ffn(i, xx, o, w_out):
        sh2, sc2, gt2 = mod[i, :, 3 * d:4 * d], mod[i, :, 4 * d:5 * d], mod[i, :, 5 * d:6 * d]
        gt1 = mod[i, :, 2 * d:3 * d]
        wup = ffn_w_up[i].astype(BF16)
        return _post_attention_ffn(xx, o, w_out.astype(BF16), gt1, g_norm_ffn[i], sh2, sc2, gt2,
                                   wup[:, :d_ff], wup[:, d_ff:], ffn_conv_w[i], ffn_conv_b[i],
                                   ffn_w_down[i].astype(BF16))

    proj = _inproj(x, g_norm_mix[0], mod[0, :, 0:d], mod[0, :, d:2 * d], _pad_cols(fox_w_in[0]).astype(BF16))
    f_t = proj[..., 3 * d:3 * d + n_heads].transpose(0, 2, 1)
    cum = _cumgate(f_t, fox_b_f[0]).reshape(b, n_heads, l // tk, 1, tk)
    ka = _kaug_prep(proj, fox_g_k[0], d, n_heads, tk, "cum", cum)
    vat = _vat_prep(proj, 2 * d, n_heads, tk)
    o = _fox_attention(proj, ka, vat, fox_g_q[0], tq, tk)
    x = ffn(0, x, o, fox_w_out[0])

    g_kv = DSA_KV_HEADS
    kvw = g_kv * dh
    proj = _inproj(x, g_norm_mix[1], mod[1, :, 0:d], mod[1, :, d:2 * d], _pad_cols(dsa_w_in[0]).astype(BF16))
    o0, o1, o2 = d, d + kvw, d + 2 * kvw
    o3 = o2 + IDX_HEADS * IDX_DIM
    o4 = o3 + IDX_DIM
    ka = _kaug_prep(proj, dsa_g_k[0], o0, g_kv, tk, "pos")
    vat = _vat_prep(proj, o1, g_kv, tk)
    ki = _kaug_prep(proj, dsa_g_kidx[0], o3, 1, tk, "none")
    o = _dsa_attention(proj, ka, vat, ki, dsa_g_q[0], n_heads, o2, o4, tq, tk, top_k)
    x = ffn(1, x, o, dsa_w_out[0])
    return x
```

```python
import functools
import math

import jax
import jax.numpy as jnp
import ml_dtypes
import numpy as np
from jax import lax
from jax.experimental import pallas as pl
from jax.experimental.pallas import tpu as pltpu

F32 = jnp.float32
BF16 = jnp.bfloat16
I32 = jnp.int32

HEAD_DIM = 64
DSA_KV_HEADS = 4
IDX_HEADS = 8
IDX_DIM = 64
TOPK_MAX = 256
EPS = 1e-6
LOG2E = 1.4426950408889634
LANES = 128
SUBLANES = 8
VMEM_LIMIT = 56 * 1024 * 1024
NEG_BIG = -1e30
F32_LOWEST = -3.4028234663852886e38
KEY_NEG_INF = -0x7F800000
KEY_POS_INF = 0x7F800000
Q_TILE = 256
FOX_Q_TILE = 512
K_TILE = 512
AUG_ROWS = SUBLANES
VA_ROWS = 80
POS_SPLIT = 256


def _params(*sem):
    return pltpu.CompilerParams(dimension_semantics=sem, vmem_limit_bytes=VMEM_LIMIT)


def _resident(shape, index_map):
    return pl.BlockSpec(shape, index_map, pipeline_mode=pl.Buffered(1))


def _round_up(n, m):
    return (n + m - 1) // m * m


def _adaln_kernel(c_ref, w_ref, b_ref, o_ref):
    c = c_ref[...]
    ca = c * jax.nn.sigmoid(c)
    o_ref[0] = jnp.dot(ca, w_ref[0], preferred_element_type=F32) + b_ref[0]


def _adaln(c, w_ada, b_ada):
    depth, d, n = w_ada.shape
    b = c.shape[0]
    rows = _round_up(b, 8)
    c_pad = jnp.pad(c, ((0, rows - b), (0, 0)))
    tn = 1536
    out = pl.pallas_call(
        _adaln_kernel,
        grid=(depth, n // tn),
        in_specs=[
            pl.BlockSpec((rows, d), lambda i, j: (0, 0)),
            pl.BlockSpec((1, d, tn), lambda i, j: (i, 0, j)),
            pl.BlockSpec((1, 1, tn), lambda i, j: (i, 0, j)),
        ],
        out_specs=pl.BlockSpec((1, rows, tn), lambda i, j: (i, 0, j)),
        out_shape=jax.ShapeDtypeStruct((depth, rows, n), F32),
        compiler_params=_params("arbitrary", "arbitrary"),
        name="adaln",
    )(c_pad, w_ada, b_ada.reshape(depth, 1, n))
    return out[:, :b]


def _modulated(x, g, sh, sc):
    ms = jnp.mean(x * x, axis=-1, keepdims=True)
    return x * lax.rsqrt(ms + EPS) * (g * (1.0 + sc)) + sh


def _inproj_kernel(x_ref, g_ref, sh_ref, sc_ref, w_ref, o_ref):
    h = _modulated(x_ref[0], g_ref[...], sh_ref[0], sc_ref[0])
    o_ref[0] = jnp.dot(h.astype(BF16), w_ref[...], preferred_element_type=F32)


def _inproj(x, g, sh, sc, w_bf16, tm=512):
    b, l, d = x.shape
    n = w_bf16.shape[1]
    tm = min(tm, l)
    return pl.pallas_call(
        _inproj_kernel,
        grid=(b, l // tm),
        in_specs=[
            pl.BlockSpec((1, tm, d), lambda i, j: (i, j, 0)),
            pl.BlockSpec((1, d), lambda i, j: (0, 0)),
            pl.BlockSpec((1, 1, d), lambda i, j: (i, 0, 0)),
            pl.BlockSpec((1, 1, d), lambda i, j: (i, 0, 0)),
            _resident((d, n), lambda i, j: (0, 0)),
        ],
        out_specs=pl.BlockSpec((1, tm, n), lambda i, j: (i, j, 0)),
        out_shape=jax.ShapeDtypeStruct((b, l, n), F32),
        compiler_params=_params("arbitrary", "arbitrary"),
        name="inproj",
    )(x, g.reshape(1, d), sh.reshape(b, 1, d), sc.reshape(b, 1, d), w_bf16)


def _split3(x):
    hi = x.astype(BF16)
    r = x - hi.astype(F32)
    mid = r.astype(BF16)
    lo = (r - mid.astype(F32)).astype(BF16)
    return hi, mid, lo


def _cumgate_kernel(f_ref, b_ref, o_ref, *, cw):
    x = f_ref[0] + b_ref[...]
    lf = jnp.minimum(x, 0.0) - jnp.log1p(jnp.exp(-jnp.abs(x)))
    h, l = lf.shape
    row = lax.broadcasted_iota(I32, (cw, cw), 0)
    col = lax.broadcasted_iota(I32, (cw, cw), 1)
    tri = jnp.where(row <= col, 1.0, 0.0).astype(BF16)
    carry = jnp.zeros((h, 1), F32)
    for c in range(l // cw):
        hi, mid, lo = _split3(lf[:, c * cw:(c + 1) * cw])
        cs = (jnp.dot(hi, tri, preferred_element_type=F32)
              + jnp.dot(mid, tri, preferred_element_type=F32)
              + jnp.dot(lo, tri, preferred_element_type=F32)) + carry
        o_ref[0, :, c * cw:(c + 1) * cw] = cs * LOG2E
        carry = cs[:, cw - 1:cw]


def _cumgate(f_t, b_f):
    b, h, l = f_t.shape
    cw = min(256, l)
    return pl.pallas_call(
        functools.partial(_cumgate_kernel, cw=cw),
        grid=(b,),
        in_specs=[
            pl.BlockSpec((1, h, l), lambda i: (i, 0, 0)),
            pl.BlockSpec((h, 1), lambda i: (0, 0)),
        ],
        out_specs=pl.BlockSpec((1, h, l), lambda i: (i, 0, 0)),
        out_shape=jax.ShapeDtypeStruct((b, h, l), F32),
        compiler_params=_params("arbitrary"),
        name="cumgate",
    )(f_t, b_f.reshape(h, 1))


def _norm_rows(xh, g, extra_scale=1.0):
    ms = jnp.mean(xh * xh, axis=0, keepdims=True)
    return xh * lax.rsqrt(ms + EPS) * (g * extra_scale)


def _kaug_prep_kernel(*refs, n_heads, mode, tk):
    if mode == "cum":
        x_ref, g_ref, c_ref, o_ref = refs
    else:
        x_ref, g_ref, o_ref = refs
    j = pl.program_id(1)
    g = g_ref[...]
    r8 = lax.broadcasted_iota(I32, (AUG_ROWS, tk), 0)
    pad = jnp.zeros((LANES - HEAD_DIM - AUG_ROWS, tk), F32)
    if mode == "pos":
        kpos = lax.broadcasted_iota(I32, (AUG_ROWS, tk), 1) + j * tk
        a = (kpos // POS_SPLIT).astype(F32)
        bb = (kpos % POS_SPLIT).astype(F32)
        aug_pos = jnp.where(r8 < 3, a, jnp.where(r8 < 6, bb, 0.0))
    for hp in range((n_heads + 1) // 2):
        xt = x_ref[0, :, hp * LANES:(hp + 1) * LANES].T
        for hh in range(min(2, n_heads - 2 * hp)):
            h = 2 * hp + hh
            kn = _norm_rows(xt[hh * HEAD_DIM:(hh + 1) * HEAD_DIM], g)
            if mode == "cum":
                hi, mid, lo = _split3(c_ref[0, h, 0])
                aug = jnp.where(r8 == 0, hi.astype(F32),
                                jnp.where(r8 == 1, mid.astype(F32), jnp.where(r8 == 2, lo.astype(F32), 0.0)))
            elif mode == "pos":
                aug = aug_pos
            else:
                aug = jnp.zeros((AUG_ROWS, tk), F32)
            t = jnp.concatenate([kn, aug, pad], axis=0)
            o_ref[0, h, 0] = t.T.astype(o_ref.dtype)


def _kaug_prep(proj, g, col0, n_heads, tk, mode, cum=None):
    b, l, _ = proj.shape
    w = max(LANES, n_heads * HEAD_DIM)
    assert col0 % w == 0
    nk = l // tk
    in_specs = [
        pl.BlockSpec((1, tk, w), lambda i, j: (i, j, col0 // w)),
        pl.BlockSpec((HEAD_DIM, 1), lambda i, j: (0, 0)),
    ]
    args = [proj, g.reshape(HEAD_DIM, 1)]
    if mode == "cum":
        in_specs.append(pl.BlockSpec((1, n_heads, 1, 1, tk), lambda i, j: (i, 0, j, 0, 0)))
        args.append(cum)
    return pl.pallas_call(
        functools.partial(_kaug_prep_kernel, n_heads=n_heads, mode=mode, tk=tk),
        grid=(b, nk),
        in_specs=in_specs,
        out_specs=pl.BlockSpec((1, n_heads, 1, tk, LANES), lambda i, j: (i, 0, j, 0, 0)),
        out_shape=jax.ShapeDtypeStruct((b, n_heads, nk, tk, LANES), BF16),
        compiler_params=_params("arbitrary", "arbitrary"),
        name="kaug_prep",
    )(*args)


def _vat_prep_kernel(x_ref, o_ref, *, n_heads, tk):
    r16 = lax.broadcasted_iota(I32, (VA_ROWS - HEAD_DIM, tk), 0)
    ones_rows = jnp.where(r16 == 0, 1.0, 0.0).astype(o_ref.dtype)
    for hp in range(n_heads // 2):
        xt = x_ref[0, :, hp * LANES:(hp + 1) * LANES].T
        for hh in range(2):
            h = 2 * hp + hh
            o_ref[0, h, 0, :HEAD_DIM] = xt[hh * HEAD_DIM:(hh + 1) * HEAD_DIM].astype(o_ref.dtype)
            o_ref[0, h, 0, HEAD_DIM:] = ones_rows


def _vat_prep(proj, col0, n_heads, tk):
    b, l, _ = proj.shape
    w = n_heads * HEAD_DIM
    assert col0 % w == 0 and n_heads % 2 == 0
    nk = l // tk
    return pl.pallas_call(
        functools.partial(_vat_prep_kernel, n_heads=n_heads, tk=tk),
        grid=(b, nk),
        in_specs=[pl.BlockSpec((1, tk, w), lambda i, j: (i, j, col0 // w))],
        out_specs=pl.BlockSpec((1, n_heads, 1, VA_ROWS, tk), lambda i, j: (i, 0, j, 0, 0)),
        out_shape=jax.ShapeDtypeStruct((b, n_heads, nk, VA_ROWS, tk), BF16),
        compiler_params=_params("arbitrary", "arbitrary"),
        name="vat_prep",
    )(proj)


def _col_reduce(x, op):
    rows, n = x.shape
    ways = 8 if rows % (8 * SUBLANES) == 0 else 1
    x = op(x.reshape(ways, rows // (ways * SUBLANES), SUBLANES, n), axis=1)
    return op(op(x, axis=0), axis=0, keepdims=True)


def _softmax_step_t(s, m_ref, acc_ref, vat, idx):
    m_prev = m_ref[idx]
    m_new = jnp.maximum(m_prev, _col_reduce(s, jnp.max))
    p = jnp.exp2(s - m_new).astype(BF16)
    alpha = jnp.exp2(m_prev - m_new)
    acc_ref[idx] = alpha * acc_ref[idx] + jnp.dot(vat, p, preferred_element_type=F32)
    m_ref[idx] = m_new


def _pipelined_chunks(nch, qk, process):
    pairs = (nch - 1) // 2
    rest = nch - 2 * pairs
    qk(0, 0)

    def body(jj, carry):
        j = 2 * jj
        qk(j + 1, 1)
        process(j, 0, False)
        qk(j + 2, 0)
        process(j + 1, 1, False)
        return carry

    lax.fori_loop(0, pairs, body, 0)

    @pl.when(rest == 1)
    def _():
        process(2 * pairs, 0, True)

    @pl.when(rest == 2)
    def _():
        qk(2 * pairs + 1, 1)
        process(2 * pairs, 0, False)
        process(2 * pairs + 1, 1, True)


def _query_weights(q_t, g, scale, aug):
    qn = _norm_rows(q_t, g, scale * LOG2E)
    pad = jnp.zeros((LANES - HEAD_DIM - AUG_ROWS, q_t.shape[1]), F32)
    return jnp.concatenate([qn, aug, pad], axis=0).astype(BF16)


def _fox_kernel(q_ref, ka_ref, vat_ref, gq_ref, o_ref, s_ref, m_ref, acc_ref, *, tq, tk, hb, scale):
    qi = pl.program_id(2)
    gq = gq_ref[...]
    r8 = lax.broadcasted_iota(I32, (AUG_ROWS, tq), 0)
    aug = jnp.where(r8 < 3, -1.0, 0.0)
    q_t = q_ref[0].T
    ws = [_query_weights(q_t[h * HEAD_DIM:(h + 1) * HEAD_DIM], gq, scale, aug) for h in range(hb)]
    m_ref[...] = jnp.full(m_ref.shape, -jnp.inf, F32)
    acc_ref[...] = jnp.zeros(acc_ref.shape, F32)
    nch = ((qi + 1) * tq + tk - 1) // tk

    def qk(j, slot):
        for h in range(hb):
            s_ref[slot, h] = jnp.dot(ka_ref[0, h, j], ws[h], preferred_element_type=F32)

    def process(j, slot, last):
        for h in range(hb):
            s = s_ref[slot, h]
            if last:
                k_pos = lax.broadcasted_iota(I32, (tk, tq), 0) + j * tk
                q_pos = lax.broadcasted_iota(I32, (tk, tq), 1) + qi * tq
                s = jnp.where(k_pos <= q_pos, s, -jnp.inf)
            _softmax_step_t(s, m_ref, acc_ref, vat_ref[0, h, j], h)

    _pipelined_chunks(nch, qk, process)
    outs = []
    for h in range(hb):
        acc = acc_ref[h]
        outs.append(acc[:HEAD_DIM] / acc[HEAD_DIM:HEAD_DIM + 1])
    o_ref[0] = jnp.concatenate(outs, axis=0).T.astype(o_ref.dtype)


def _fox_attention(proj, ka, vat, g_q, tq, tk, hb=8):
    b, l, _ = proj.shape
    h, nk = ka.shape[1], ka.shape[2]
    dh = HEAD_DIM
    hb = math.gcd(hb, h)
    return pl.pallas_call(
        functools.partial(_fox_kernel, tq=tq, tk=tk, hb=hb, scale=dh ** -0.5),
        grid=(b, h // hb, l // tq),
        in_specs=[
            pl.BlockSpec((1, tq, hb * dh), lambda i, j, k: (i, k, j)),
            _resident((1, hb, nk, tk, LANES), lambda i, j, k: (i, j, 0, 0, 0)),
            _resident((1, hb, nk, VA_ROWS, tk), lambda i, j, k: (i, j, 0, 0, 0)),
            pl.BlockSpec((dh, 1), lambda i, j, k: (0, 0)),
        ],
        out_specs=pl.BlockSpec((1, tq, hb * dh), lambda i, j, k: (i, k, j)),
        out_shape=jax.ShapeDtypeStruct((b, l, h * dh), BF16),
        scratch_shapes=[
            pltpu.VMEM((2, hb, tk, tq), F32),
            pltpu.VMEM((hb, 1, tq), F32),
            pltpu.VMEM((hb, VA_ROWS, tq), F32),
        ],
        compiler_params=_params("arbitrary", "arbitrary", "arbitrary"),
        name="fox_attn",
    )(proj, ka, vat, g_q.reshape(dh, 1))


def _ffn_kernel(x_ref, o_ref, wo_ref, gt1_ref, g_ref, sh_ref, sc_ref, gt2_ref,
                wa_ref, wg_ref, cw_ref, cb_ref, wd_ref, out_ref, abuf_ref, *, tm, fc):
    i = pl.program_id(1)
    y = jnp.dot(o_ref[0], wo_ref[...], preferred_element_type=F32)
    x1 = x_ref[0] + gt1_ref[0] * y
    h = _modulated(x1, g_ref[...], sh_ref[0], sc_ref[0]).astype(BF16)

    @pl.when(i == 0)
    def _():
        abuf_ref[0:8, :] = jnp.zeros((8, abuf_ref.shape[1]), F32)

    @pl.when(i > 0)
    def _():
        abuf_ref[0:8, :] = abuf_ref[tm:tm + 8, :]

    f = abuf_ref.shape[1]
    y2 = None
    for c in range(f // fc):
        sl = slice(c * fc, (c + 1) * fc)
        a = jnp.dot(h, wa_ref[:, sl], preferred_element_type=F32)
        gate = jnp.dot(h, wg_ref[:, sl], preferred_element_type=F32)
        abuf_ref[8:8 + tm, sl] = a
        a1 = abuf_ref[7:7 + tm, sl]
        a2 = abuf_ref[6:6 + tm, sl]
        ac = cw_ref[0:1, sl] * a2 + cw_ref[1:2, sl] * a1 + cw_ref[2:3, sl] * a + cb_ref[:, sl]
        act = (ac * jax.nn.sigmoid(ac) * gate).astype(BF16)
        part = jnp.dot(act, wd_ref[sl, :], preferred_element_type=F32)
        y2 = part if y2 is None else y2 + part
    out_ref[0] = x1 + gt2_ref[0] * y2


def _post_attention_ffn(x, o, wo, gt1, g, sh, sc, gt2, wa, wg, cw, cb, wd, tm=512):
    b, l, d = x.shape
    f = wa.shape[1]
    tm = min(tm, l)
    row = lambda i, j: (i, j, 0)
    per_batch = lambda i, j: (i, 0, 0)
    const = lambda i, j: (0, 0)
    return pl.pallas_call(
        functools.partial(_ffn_kernel, tm=tm, fc=math.gcd(f, 256)),
        grid=(b, l // tm),
        in_specs=[
            pl.BlockSpec((1, tm, d), row),
            pl.BlockSpec((1, tm, d), row),
            _resident((d, d), const),
            pl.BlockSpec((1, 1, d), per_batch),
            pl.BlockSpec((1, d), const),
            pl.BlockSpec((1, 1, d), per_batch),
            pl.BlockSpec((1, 1, d), per_batch),
            pl.BlockSpec((1, 1, d), per_batch),
            _resident((d, f), const),
            _resident((d, f), const),
            pl.BlockSpec((3, f), const),
            pl.BlockSpec((1, f), const),
            _resident((f, d), const),
        ],
        out_specs=pl.BlockSpec((1, tm, d), row),
        out_shape=jax.ShapeDtypeStruct((b, l, d), F32),
        scratch_shapes=[pltpu.VMEM((tm + 8, f), F32)],
        compiler_params=_params("arbitrary", "arbitrary"),
        name="post_attn_ffn",
    )(x, o, wo, gt1.reshape(b, 1, d), g.reshape(1, d), sh.reshape(b, 1, d), sc.reshape(b, 1, d),
      gt2.reshape(b, 1, d), wa, wg, cw, cb.reshape(1, f), wd)


def _key_to_float(key):
    bits = jnp.where(key < 0, jnp.int32(-2 ** 31) - key, key)
    return lax.bitcast_convert_type(bits, F32)


def _bf16_pieces(x):
    out, r = [], float(x)
    for _ in range(3):
        p = float(np.asarray(r, dtype=ml_dtypes.bfloat16).astype(np.float32))
        out.append(p)
        r -= p
    return tuple(out)


def _dsa_kernel(q_ref, qi_ref, wi_ref, ka_ref, vat_ref, ki_ref, gq_ref, o_ref,
                s_ref, b_ref, p_ref, m_ref, acc_ref, *, tq, tk, top_k, scale, slope_pieces):
    qt = pl.program_id(1)
    n_groups = ka_ref.shape[1]
    group = q_ref.shape[2] // HEAD_DIM // n_groups
    nch = ((qt + 1) * tq + tk - 1) // tk
    k_off = lax.broadcasted_iota(I32, (tk, tq), 0)
    q_pos = lax.broadcasted_iota(I32, (tk, tq), 1) + qt * tq
    kf = float(top_k)

    qi_t = qi_ref[0].T
    zpad = jnp.zeros((LANES - IDX_DIM, tq), F32)
    wqs = [jnp.concatenate([qi_t[hh * IDX_DIM:(hh + 1) * IDX_DIM], zpad], axis=0).astype(BF16)
           for hh in range(IDX_HEADS)]
    w_t = wi_ref[0].T[IDX_DIM:IDX_DIM + IDX_HEADS] * (IDX_HEADS ** -0.5 * IDX_DIM ** -0.5)

    def score_chunk(j, carry):
        kc = ki_ref[0, 0, j]
        sc = jnp.zeros((tk, tq), F32)
        for hh in range(IDX_HEADS):
            rel = jnp.maximum(jnp.dot(kc, wqs[hh], preferred_element_type=F32), 0.0)
            sc = sc + rel * w_t[hh:hh + 1]
        sc = jnp.where(k_off + j * tk <= q_pos, sc, -jnp.inf)
        s_ref[j] = sc
        b_ref[j] = sc.astype(BF16)
        return carry

    lax.fori_loop(0, nch, score_chunk, 0)

    def count(pred):
        ways = 8 if tk % (8 * SUBLANES) == 0 else 1

        def body(j, c):
            x = jnp.where(pred(s_ref[j], j), 1.0, 0.0)
            return c + x.reshape(ways, tk // (ways * SUBLANES), SUBLANES, tq).sum(axis=1)
        c = lax.fori_loop(0, nch, body, jnp.zeros((ways, SUBLANES, tq), F32))
        return jnp.sum(jnp.sum(c, axis=0), axis=0, keepdims=True)

    def count_coarse(c16):
        rows = 2 * SUBLANES
        ways = 8 if tk % (8 * rows) == 0 else 1
        one, zero = jnp.ones((), BF16), jnp.zeros((), BF16)

        def body(j, c):
            x = jnp.where(b_ref[j] >= c16, one, zero).reshape(ways, tk // (ways * rows), rows, tq)
            for i in range(x.shape[1]):
                c = c + x[:, i]
            return c
        c = lax.fori_loop(0, nch, body, jnp.zeros((ways, rows, tq), BF16))
        return jnp.sum(jnp.sum(c.astype(F32), axis=0), axis=0, keepdims=True)

    def search(_):
        def coarse_step(_, lohi):
            lo, hi = lohi
            mid = (lo + hi) >> 1
            ge = count_coarse(_key_to_float(mid << 16).astype(BF16)) >= kf
            return jnp.where(ge, mid, lo), jnp.where(ge, hi, mid)

        lo16, hi16 = lax.fori_loop(0, 16, coarse_step, (jnp.full((1, tq), KEY_NEG_INF >> 16, I32),
                                                        jnp.full((1, tq), (KEY_POS_INF >> 16) + 1, I32)))

        def flag(lo, hi, clo):
            return jnp.max(jnp.where((clo != kf) & (hi > lo + 1), 1.0, 0.0))

        def cond(st):
            return st[3] > 0.0

        def step(st):
            lo, hi, clo, _ = st
            active = (clo != kf) & (hi > lo + 1)
            mid = (lo >> 1) + (hi >> 1) + (lo & hi & 1)
            cmid = _key_to_float(mid)
            cnt = count(lambda s, j: s >= cmid)
            ge = cnt >= kf
            take_lo = active & ge
            take_hi = active & jnp.logical_not(ge)
            lo = jnp.where(take_lo, mid, lo)
            clo = jnp.where(take_lo, cnt, clo)
            hi = jnp.where(take_hi, mid, hi)
            return lo, hi, clo, flag(lo, hi, clo)

        lo0 = jnp.maximum((lo16 - 1) << 16, KEY_NEG_INF)
        hi0 = hi16 << 16
        f_lo0 = _key_to_float(lo0)
        clo0 = count(lambda s, j: s >= f_lo0)
        lo, _, clo, _ = lax.while_loop(cond, step, (lo0, hi0, clo0, flag(lo0, hi0, clo0)))
        return _key_to_float(lo), clo

    def no_search(_):
        return jnp.full((1, tq), -jnp.inf, F32), jnp.full((1, tq), kf, F32)

    tau, n_ge = lax.cond((qt + 1) * tq > top_k, search, no_search, 0)
    tau_sel = jnp.maximum(tau, F32_LOWEST)

    tie = (n_ge > kf) & (tau > -jnp.inf)
    any_tie = jnp.max(jnp.where(tie, 1.0, 0.0)) > 0.0

    def mask_plain(_):
        def body(j, carry):
            s_ref[j] = jnp.where(s_ref[j] >= tau_sel, 0.0, NEG_BIG)
            return carry
        lax.fori_loop(0, nch, body, 0)
        return 0

    def mask_ties(_):
        need = kf - count(lambda s, j: s > tau)

        def step(_, lohi):
            lo, hi = lohi
            mid = (lo + hi) >> 1
            ge = count(lambda s, j: (s == tau) & (k_off + j * tk <= mid)) >= need
            return jnp.where(ge, lo, mid), jnp.where(ge, mid, hi)

        steps = max(1, int(math.ceil(math.log2(s_ref.shape[0] * tk + 1))))
        lo0 = jnp.full((1, tq), -1, I32)
        hi0 = jnp.full((1, tq), 1, I32) * (nch * tk - 1)
        _, hi = lax.fori_loop(0, steps, step, (lo0, hi0))
        jstar = jnp.where(tie, hi, jnp.int32(2 ** 30))

        def body(j, carry):
            s = s_ref[j]
            eq_bias = jnp.where(k_off + j * tk <= jstar, 0.0, NEG_BIG)
            s_ref[j] = jnp.where(s > tau_sel, 0.0, jnp.where(s == tau_sel, eq_bias, NEG_BIG))
            return carry
        lax.fori_loop(0, nch, body, 0)
        return 0

    lax.cond(any_tie, mask_ties, mask_plain, 0)

    gq = gq_ref[...]
    q_t = q_ref[0].T
    r8 = lax.broadcasted_iota(I32, (AUG_ROWS, tq), 0)
    hb = m_ref.shape[0]
    for h0 in range(0, n_groups * group, hb):
        heads = list(range(h0, h0 + hb))
        ws = []
        for hq in heads:
            hi, mid, lo = slope_pieces[hq]
            rows = (POS_SPLIT * hi, POS_SPLIT * mid, POS_SPLIT * lo, hi, mid, lo)
            aug = jnp.zeros((AUG_ROWS, tq), F32)
            for i, val in enumerate(rows):
                aug = jnp.where(r8 == i, val, aug)
            ws.append(_query_weights(q_t[hq * HEAD_DIM:(hq + 1) * HEAD_DIM], gq, scale, aug))
        m_ref[...] = jnp.full(m_ref.shape, -jnp.inf, F32)
        acc_ref[...] = jnp.zeros(acc_ref.shape, F32)

        def qk(j, slot, heads=heads, ws=ws):
            for i, hq in enumerate(heads):
                p_ref[slot, i] = jnp.dot(ka_ref[0, hq // group, j], ws[i], preferred_element_type=F32) + s_ref[j]

        def process(j, slot, last, heads=heads):
            for i, hq in enumerate(heads):
                _softmax_step_t(p_ref[slot, i], m_ref, acc_ref, vat_ref[0, hq // group, j], i)

        _pipelined_chunks(nch, qk, process)
        outs = []
        for i in range(hb):
            acc = acc_ref[i]
            outs.append(acc[:HEAD_DIM] / acc[HEAD_DIM:HEAD_DIM + 1])
        o_ref[0, :, h0 * HEAD_DIM:(h0 + hb) * HEAD_DIM] = jnp.concatenate(outs, axis=0).T.astype(o_ref.dtype)


def _dsa_attention(proj, ka, vat, ki, g_q, n_q_heads, qi_col0, wi_col0, tq, tk, top_k):
    b, l, _ = proj.shape
    dh = HEAD_DIM
    n_groups, nk = ka.shape[1], ka.shape[2]
    qw = n_q_heads * dh
    iw = IDX_HEADS * IDX_DIM
    assert qi_col0 % iw == 0 and wi_col0 % LANES == IDX_DIM
    slope_pieces = tuple(_bf16_pieces(2.0 ** (-8.0 * (i + 1) / n_q_heads) * LOG2E) for i in range(n_q_heads))
    hb = math.gcd(8, n_q_heads)
    return pl.pallas_call(
        functools.partial(_dsa_kernel, tq=tq, tk=tk, top_k=top_k, scale=dh ** -0.5, slope_pieces=slope_pieces),
        grid=(b, l // tq),
        in_specs=[
            pl.BlockSpec((1, tq, qw), lambda i, j: (i, j, 0)),
            pl.BlockSpec((1, tq, iw), lambda i, j: (i, j, qi_col0 // iw)),
            pl.BlockSpec((1, tq, LANES), lambda i, j: (i, j, wi_col0 // LANES)),
            _resident((1, n_groups, nk, tk, LANES), lambda i, j: (i, 0, 0, 0, 0)),
            _resident((1, n_groups, nk, VA_ROWS, tk), lambda i, j: (i, 0, 0, 0, 0)),
            _resident((1, 1, nk, tk, LANES), lambda i, j: (i, 0, 0, 0, 0)),
            pl.BlockSpec((dh, 1), lambda i, j: (0, 0)),
        ],
        out_specs=pl.BlockSpec((1, tq, qw), lambda i, j: (i, j, 0)),
        out_shape=jax.ShapeDtypeStruct((b, l, qw), BF16),
        scratch_shapes=[
            pltpu.VMEM((nk, tk, tq), F32),
            pltpu.VMEM((nk, tk, tq), BF16),
            pltpu.VMEM((2, hb, tk, tq), F32),
            pltpu.VMEM((hb, 1, tq), F32),
            pltpu.VMEM((hb, VA_ROWS, tq), F32),
        ],
        compiler_params=_params("arbitrary", "arbitrary"),
        name="dsa_attn",
    )(proj, proj, proj, ka, vat, ki, g_q.reshape(dh, 1))


def _pad_cols(w, mult=LANES):
    n = w.shape[1]
    return jnp.pad(w, ((0, 0), (0, _round_up(n, mult) - n)))


def kernel(x, c, w_ada, b_ada, g_norm_mix, g_norm_ffn, fox_w_in, fox_b_f, fox_g_q, fox_g_k, fox_w_out,
           dsa_w_in, dsa_g_q, dsa_g_k, dsa_g_kidx, dsa_w_out, ffn_w_up, ffn_conv_w, ffn_conv_b, ffn_w_down):
    b, l, d = x.shape
    dh = HEAD_DIM
    n_heads = d // dh
    d_ff = ffn_w_down.shape[1]
    top_k = min(TOPK_MAX, l // 4)
    tq = min(Q_TILE, l)
    tk = min(K_TILE, l)

    mod = _adaln(c, w_ada, b_ada)

    def ffn(i, xx, o, w_out):
        sh2, sc2, gt2 = mod[i, :, 3 * d:4 * d], mod[i, :, 4 * d:5 * d], mod[i, :, 5 * d:6 * d]
        gt1 = mod[i, :, 2 * d:3 * d]
        wup = ffn_w_up[i].astype(BF16)
        return _post_attention_ffn(xx, o, w_out.astype(BF16), gt1, g_norm_ffn[i], sh2, sc2, gt2,
                                   wup[:, :d_ff], wup[:, d_ff:], ffn_conv_w[i], ffn_conv_b[i],
                                   ffn_w_down[i].astype(BF16))

    proj = _inproj(x, g_norm_mix[0], mod[0, :, 0:d], mod[0, :, d:2 * d], _pad_cols(fox_w_in[0]).astype(BF16))
    f_t = proj[..., 3 * d:3 * d + n_heads].transpose(0, 2, 1)
    cum = _cumgate(f_t, fox_b_f[0]).reshape(b, n_heads, l // tk, 1, tk)
    ka = _kaug_prep(proj, fox_g_k[0], d, n_heads, tk, "cum", cum)
    vat = _vat_prep(proj, 2 * d, n_heads, tk)
    o = _fox_attention(proj, ka, vat, fox_g_q[0], min(FOX_Q_TILE, l), tk)
    x = ffn(0, x, o, fox_w_out[0])

    g_kv = DSA_KV_HEADS
    kvw = g_kv * dh
    proj = _inproj(x, g_norm_mix[1], mod[1, :, 0:d], mod[1, :, d:2 * d], _pad_cols(dsa_w_in[0]).astype(BF16))
    o0, o1, o2 = d, d + kvw, d + 2 * kvw
    o3 = o2 + IDX_HEADS * IDX_DIM
    o4 = o3 + IDX_DIM
    ka = _kaug_prep(proj, dsa_g_k[0], o0, g_kv, tk, "pos")
    vat = _vat_prep(proj, o1, g_kv, tk)
    ki = _kaug_prep(proj, dsa_g_kidx[0], o3, 1, tk, "none")
    o = _dsa_attention(proj, ka, vat, ki, dsa_g_q[0], n_heads, o2, o4, tq, tk, top_k)
    x = ffn(1, x, o, dsa_w_out[0])
    return x
```

```python
import functools
import math

import jax
import jax.numpy as jnp
import ml_dtypes
import numpy as np
from jax import lax
from jax.experimental import pallas as pl
from jax.experimental.pallas import tpu as pltpu

F32 = jnp.float32
BF16 = jnp.bfloat16
I32 = jnp.int32

HEAD_DIM = 64
DSA_KV_HEADS = 4
IDX_HEADS = 8
IDX_DIM = 64
TOPK_MAX = 256
EPS = 1e-6
LOG2E = 1.4426950408889634
LANES = 128
SUBLANES = 8
VMEM_LIMIT = 56 * 1024 * 1024
NEG_BIG = -1e30
F32_LOWEST = -3.4028234663852886e38
KEY_NEG_INF = -0x7F800000
KEY_POS_INF = 0x7F800000
Q_TILE = 256
FOX_Q_TILE = 512
K_TILE = 512
AUG_ROWS = SUBLANES
VA_ROWS = 80
POS_SPLIT = 256


def _params(*sem):
    return pltpu.CompilerParams(dimension_semantics=sem, vmem_limit_bytes=VMEM_LIMIT)


def _resident(shape, index_map):
    return pl.BlockSpec(shape, index_map, pipeline_mode=pl.Buffered(1))


def _round_up(n, m):
    return (n + m - 1) // m * m


def _adaln_kernel(c_ref, w_ref, b_ref, o_ref):
    c = c_ref[...]
    ca = c * jax.nn.sigmoid(c)
    o_ref[0] = jnp.dot(ca, w_ref[0], preferred_element_type=F32) + b_ref[0]


def _adaln(c, w_ada, b_ada):
    depth, d, n = w_ada.shape
    b = c.shape[0]
    rows = _round_up(b, 8)
    c_pad = jnp.pad(c, ((0, rows - b), (0, 0)))
    tn = 1536
    out = pl.pallas_call(
        _adaln_kernel,
        grid=(depth, n // tn),
        in_specs=[
            pl.BlockSpec((rows, d), lambda i, j: (0, 0)),
            pl.BlockSpec((1, d, tn), lambda i, j: (i, 0, j)),
            pl.BlockSpec((1, 1, tn), lambda i, j: (i, 0, j)),
        ],
        out_specs=pl.BlockSpec((1, rows, tn), lambda i, j: (i, 0, j)),
        out_shape=jax.ShapeDtypeStruct((depth, rows, n), F32),
        compiler_params=_params("arbitrary", "arbitrary"),
        name="adaln",
    )(c_pad, w_ada, b_ada.reshape(depth, 1, n))
    return out[:, :b]


def _modulated(x, g, sh, sc):
    ms = jnp.mean(x * x, axis=-1, keepdims=True)
    return x * lax.rsqrt(ms + EPS) * (g * (1.0 + sc)) + sh


def _inproj_kernel(x_ref, g_ref, sh_ref, sc_ref, w_ref, o_ref):
    h = _modulated(x_ref[0], g_ref[...], sh_ref[0], sc_ref[0])
    o_ref[0] = jnp.dot(h.astype(BF16), w_ref[...], preferred_element_type=F32)


def _inproj(x, g, sh, sc, w_bf16, tm=512):
    b, l, d = x.shape
    n = w_bf16.shape[1]
    tm = min(tm, l)
    return pl.pallas_call(
        _inproj_kernel,
        grid=(b, l // tm),
        in_specs=[
            pl.BlockSpec((1, tm, d), lambda i, j: (i, j, 0)),
            pl.BlockSpec((1, d), lambda i, j: (0, 0)),
            pl.BlockSpec((1, 1, d), lambda i, j: (i, 0, 0)),
            pl.BlockSpec((1, 1, d), lambda i, j: (i, 0, 0)),
            _resident((d, n), lambda i, j: (0, 0)),
        ],
        out_specs=pl.BlockSpec((1, tm, n), lambda i, j: (i, j, 0)),
        out_shape=jax.ShapeDtypeStruct((b, l, n), F32),
        compiler_params=_params("arbitrary", "arbitrary"),
        name="inproj",
    )(x, g.reshape(1, d), sh.reshape(b, 1, d), sc.reshape(b, 1, d), w_bf16)


def _split3(x):
    hi = x.astype(BF16)
    r = x - hi.astype(F32)
    mid = r.astype(BF16)
    lo = (r - mid.astype(F32)).astype(BF16)
    return hi, mid, lo


def _cumgate_kernel(f_ref, b_ref, o_ref, *, cw):
    x = f_ref[0] + b_ref[...]
    lf = jnp.minimum(x, 0.0) - jnp.log1p(jnp.exp(-jnp.abs(x)))
    h, l = lf.shape
    row = lax.broadcasted_iota(I32, (cw, cw), 0)
    col = lax.broadcasted_iota(I32, (cw, cw), 1)
    tri = jnp.where(row <= col, 1.0, 0.0).astype(BF16)
    carry = jnp.zeros((h, 1), F32)
    for c in range(l // cw):
        hi, mid, lo = _split3(lf[:, c * cw:(c + 1) * cw])
        cs = (jnp.dot(hi, tri, preferred_element_type=F32)
              + jnp.dot(mid, tri, preferred_element_type=F32)
              + jnp.dot(lo, tri, preferred_element_type=F32)) + carry
        o_ref[0, :, c * cw:(c + 1) * cw] = cs * LOG2E
        carry = cs[:, cw - 1:cw]


def _cumgate(f_t, b_f):
    b, h, l = f_t.shape
    cw = min(256, l)
    return pl.pallas_call(
        functools.partial(_cumgate_kernel, cw=cw),
        grid=(b,),
        in_specs=[
            pl.BlockSpec((1, h, l), lambda i: (i, 0, 0)),
            pl.BlockSpec((h, 1), lambda i: (0, 0)),
        ],
        out_specs=pl.BlockSpec((1, h, l), lambda i: (i, 0, 0)),
        out_shape=jax.ShapeDtypeStruct((b, h, l), F32),
        compiler_params=_params("arbitrary"),
        name="cumgate",
    )(f_t, b_f.reshape(h, 1))


def _norm_rows(xh, g, extra_scale=1.0):
    ms = jnp.mean(xh * xh, axis=0, keepdims=True)
    return xh * lax.rsqrt(ms + EPS) * (g * extra_scale)


def _kaug_prep_kernel(*refs, n_heads, mode, tk):
    if mode == "cum":
        x_ref, g_ref, c_ref, o_ref = refs
    else:
        x_ref, g_ref, o_ref = refs
    j = pl.program_id(1)
    g = g_ref[...]
    r8 = lax.broadcasted_iota(I32, (AUG_ROWS, tk), 0)
    pad = jnp.zeros((LANES - HEAD_DIM - AUG_ROWS, tk), F32)
    if mode == "pos":
        kpos = lax.broadcasted_iota(I32, (AUG_ROWS, tk), 1) + j * tk
        a = (kpos // POS_SPLIT).astype(F32)
        bb = (kpos % POS_SPLIT).astype(F32)
        aug_pos = jnp.where(r8 < 3, a, jnp.where(r8 < 6, bb, 0.0))
    for hp in range((n_heads + 1) // 2):
        xt = x_ref[0, :, hp * LANES:(hp + 1) * LANES].T
        for hh in range(min(2, n_heads - 2 * hp)):
            h = 2 * hp + hh
            kn = _norm_rows(xt[hh * HEAD_DIM:(hh + 1) * HEAD_DIM], g)
            if mode == "cum":
                hi, mid, lo = _split3(c_ref[0, h, 0])
                aug = jnp.where(r8 == 0, hi.astype(F32),
                                jnp.where(r8 == 1, mid.astype(F32), jnp.where(r8 == 2, lo.astype(F32), 0.0)))
            elif mode == "pos":
                aug = aug_pos
            else:
                aug = jnp.zeros((AUG_ROWS, tk), F32)
            t = jnp.concatenate([kn, aug, pad], axis=0)
            o_ref[0, h, 0] = t.T.astype(o_ref.dtype)


def _kaug_prep(proj, g, col0, n_heads, tk, mode, cum=None):
    b, l, _ = proj.shape
    w = max(LANES, n_heads * HEAD_DIM)
    assert col0 % w == 0
    nk = l // tk
    in_specs = [
        pl.BlockSpec((1, tk, w), lambda i, j: (i, j, col0 // w)),
        pl.BlockSpec((HEAD_DIM, 1), lambda i, j: (0, 0)),
    ]
    args = [proj, g.reshape(HEAD_DIM, 1)]
    if mode == "cum":
        in_specs.append(pl.BlockSpec((1, n_heads, 1, 1, tk), lambda i, j: (i, 0, j, 0, 0)))
        args.append(cum)
    return pl.pallas_call(
        functools.partial(_kaug_prep_kernel, n_heads=n_heads, mode=mode, tk=tk),
        grid=(b, nk),
        in_specs=in_specs,
        out_specs=pl.BlockSpec((1, n_heads, 1, tk, LANES), lambda i, j: (i, 0, j, 0, 0)),
        out_shape=jax.ShapeDtypeStruct((b, n_heads, nk, tk, LANES), BF16),
        compiler_params=_params("arbitrary", "arbitrary"),
        name="kaug_prep",
    )(*args)


def _vat_prep_kernel(x_ref, o_ref, *, n_heads, tk):
    r16 = lax.broadcasted_iota(I32, (VA_ROWS - HEAD_DIM, tk), 0)
    ones_rows = jnp.where(r16 == 0, 1.0, 0.0).astype(o_ref.dtype)
    for hp in range(n_heads // 2):
        xt = x_ref[0, :, hp * LANES:(hp + 1) * LANES].T
        for hh in range(2):
            h = 2 * hp + hh
            o_ref[0, h, 0, :HEAD_DIM] = xt[hh * HEAD_DIM:(hh + 1) * HEAD_DIM].astype(o_ref.dtype)
            o_ref[0, h, 0, HEAD_DIM:] = ones_rows


def _vat_prep(proj, col0, n_heads, tk):
    b, l, _ = proj.shape
    w = n_heads * HEAD_DIM
    assert col0 % w == 0 and n_heads % 2 == 0
    nk = l // tk
    return pl.pallas_call(
        functools.partial(_vat_prep_kernel, n_heads=n_heads, tk=tk),
        grid=(b, nk),
        in_specs=[pl.BlockSpec((1, tk, w), lambda i, j: (i, j, col0 // w))],
        out_specs=pl.BlockSpec((1, n_heads, 1, VA_ROWS, tk), lambda i, j: (i, 0, j, 0, 0)),
        out_shape=jax.ShapeDtypeStruct((b, n_heads, nk, VA_ROWS, tk), BF16),
        compiler_params=_params("arbitrary", "arbitrary"),
        name="vat_prep",
    )(proj)


def _col_reduce(x, op):
    rows, n = x.shape
    ways = 8 if rows % (8 * SUBLANES) == 0 else 1
    x = op(x.reshape(ways, rows // (ways * SUBLANES), SUBLANES, n), axis=1)
    return op(op(x, axis=0), axis=0, keepdims=True)


def _softmax_step_t(s, m_ref, acc_ref, vat, idx):
    m_prev = m_ref[idx]
    m_new = jnp.maximum(m_prev, _col_reduce(s, jnp.max))
    p = jnp.exp2(s - m_new).astype(BF16)
    alpha = jnp.exp2(m_prev - m_new)
    acc_ref[idx] = alpha * acc_ref[idx] + jnp.dot(vat, p, preferred_element_type=F32)
    m_ref[idx] = m_new


def _pipelined_chunks(nch, qk, process):
    pairs = (nch - 1) // 2
    rest = nch - 2 * pairs
    qk(0, 0)

    def body(jj, carry):
        j = 2 * jj
        qk(j + 1, 1)
        process(j, 0, False)
        qk(j + 2, 0)
        process(j + 1, 1, False)
        return carry

    lax.fori_loop(0, pairs, body, 0)

    @pl.when(rest == 1)
    def _():
        process(2 * pairs, 0, True)

    @pl.when(rest == 2)
    def _():
        qk(2 * pairs + 1, 1)
        process(2 * pairs, 0, False)
        process(2 * pairs + 1, 1, True)


def _query_weights(q_t, g, scale, aug):
    qn = _norm_rows(q_t, g, scale * LOG2E)
    pad = jnp.zeros((LANES - HEAD_DIM - AUG_ROWS, q_t.shape[1]), F32)
    return jnp.concatenate([qn, aug, pad], axis=0).astype(BF16)


def _fox_kernel(q_ref, ka_ref, vat_ref, gq_ref, o_ref, s_ref, m_ref, acc_ref, *, tq, tk, hb, scale):
    qi = pl.program_id(2)
    gq = gq_ref[...]
    r8 = lax.broadcasted_iota(I32, (AUG_ROWS, tq), 0)
    aug = jnp.where(r8 < 3, -1.0, 0.0)
    q_t = q_ref[0].T
    ws = [_query_weights(q_t[h * HEAD_DIM:(h + 1) * HEAD_DIM], gq, scale, aug) for h in range(hb)]
    m_ref[...] = jnp.full(m_ref.shape, -jnp.inf, F32)
    acc_ref[...] = jnp.zeros(acc_ref.shape, F32)
    nch = ((qi + 1) * tq + tk - 1) // tk

    def qk(j, slot):
        for h in range(hb):
            s_ref[slot, h] = jnp.dot(ka_ref[0, h, j], ws[h], preferred_element_type=F32)

    def process(j, slot, last):
        for h in range(hb):
            s = s_ref[slot, h]
            if last:
                k_pos = lax.broadcasted_iota(I32, (tk, tq), 0) + j * tk
                q_pos = lax.broadcasted_iota(I32, (tk, tq), 1) + qi * tq
                s = jnp.where(k_pos <= q_pos, s, -jnp.inf)
            _softmax_step_t(s, m_ref, acc_ref, vat_ref[0, h, j], h)

    _pipelined_chunks(nch, qk, process)
    outs = []
    for h in range(hb):
        acc = acc_ref[h]
        outs.append(acc[:HEAD_DIM] / acc[HEAD_DIM:HEAD_DIM + 1])
    o_ref[0] = jnp.concatenate(outs, axis=0).T.astype(o_ref.dtype)


def _fox_attention(proj, ka, vat, g_q, tq, tk, hb=8):
    b, l, _ = proj.shape
    h, nk = ka.shape[1], ka.shape[2]
    dh = HEAD_DIM
    hb = math.gcd(hb, h)
    return pl.pallas_call(
        functools.partial(_fox_kernel, tq=tq, tk=tk, hb=hb, scale=dh ** -0.5),
        grid=(b, h // hb, l // tq),
        in_specs=[
            pl.BlockSpec((1, tq, hb * dh), lambda i, j, k: (i, k, j)),
            _resident((1, hb, nk, tk, LANES), lambda i, j, k: (i, j, 0, 0, 0)),
            _resident((1, hb, nk, VA_ROWS, tk), lambda i, j, k: (i, j, 0, 0, 0)),
            pl.BlockSpec((dh, 1), lambda i, j, k: (0, 0)),
        ],
        out_specs=pl.BlockSpec((1, tq, hb * dh), lambda i, j, k: (i, k, j)),
        out_shape=jax.ShapeDtypeStruct((b, l, h * dh), BF16),
        scratch_shapes=[
            pltpu.VMEM((2, hb, tk, tq), F32),
            pltpu.VMEM((hb, 1, tq), F32),
            pltpu.VMEM((hb, VA_ROWS, tq), F32),
        ],
        compiler_params=_params("arbitrary", "arbitrary", "arbitrary"),
        name="fox_attn",
    )(proj, ka, vat, g_q.reshape(dh, 1))


def _ffn_kernel(x_ref, o_ref, wo_ref, gt1_ref, g_ref, sh_ref, sc_ref, gt2_ref,
                wa_ref, wg_ref, cw_ref, cb_ref, wd_ref, out_ref, abuf_ref, *, tm, fc):
    i = pl.program_id(1)
    y = jnp.dot(o_ref[0], wo_ref[...], preferred_element_type=F32)
    x1 = x_ref[0] + gt1_ref[0] * y
    h = _modulated(x1, g_ref[...], sh_ref[0], sc_ref[0]).astype(BF16)

    @pl.when(i == 0)
    def _():
        abuf_ref[0:8, :] = jnp.zeros((8, abuf_ref.shape[1]), F32)

    @pl.when(i > 0)
    def _():
        abuf_ref[0:8, :] = abuf_ref[tm:tm + 8, :]

    f = abuf_ref.shape[1]
    y2 = None
    for c in range(f // fc):
        sl = slice(c * fc, (c + 1) * fc)
        a = jnp.dot(h, wa_ref[:, sl], preferred_element_type=F32)
        gate = jnp.dot(h, wg_ref[:, sl], preferred_element_type=F32)
        abuf_ref[8:8 + tm, sl] = a
        a1 = abuf_ref[7:7 + tm, sl]
        a2 = abuf_ref[6:6 + tm, sl]
        ac = cw_ref[0:1, sl] * a2 + cw_ref[1:2, sl] * a1 + cw_ref[2:3, sl] * a + cb_ref[:, sl]
        act = (ac * jax.nn.sigmoid(ac) * gate).astype(BF16)
        part = jnp.dot(act, wd_ref[sl, :], preferred_element_type=F32)
        y2 = part if y2 is None else y2 + part
    out_ref[0] = x1 + gt2_ref[0] * y2


def _post_attention_ffn(x, o, wo, gt1, g, sh, sc, gt2, wa, wg, cw, cb, wd, tm=512):
    b, l, d = x.shape
    f = wa.shape[1]
    tm = min(tm, l)
    row = lambda i, j: (i, j, 0)
    per_batch = lambda i, j: (i, 0, 0)
    const = lambda i, j: (0, 0)
    return pl.pallas_call(
        functools.partial(_ffn_kernel, tm=tm, fc=math.gcd(f, 256)),
        grid=(b, l // tm),
        in_specs=[
            pl.BlockSpec((1, tm, d), row),
            pl.BlockSpec((1, tm, d), row),
            _resident((d, d), const),
            pl.BlockSpec((1, 1, d), per_batch),
            pl.BlockSpec((1, d), const),
            pl.BlockSpec((1, 1, d), per_batch),
            pl.BlockSpec((1, 1, d), per_batch),
            pl.BlockSpec((1, 1, d), per_batch),
            _resident((d, f), const),
            _resident((d, f), const),
            pl.BlockSpec((3, f), const),
            pl.BlockSpec((1, f), const),
            _resident((f, d), const),
        ],
        out_specs=pl.BlockSpec((1, tm, d), row),
        out_shape=jax.ShapeDtypeStruct((b, l, d), F32),
        scratch_shapes=[pltpu.VMEM((tm + 8, f), F32)],
        compiler_params=_params("arbitrary", "arbitrary"),
        name="post_attn_ffn",
    )(x, o, wo, gt1.reshape(b, 1, d), g.reshape(1, d), sh.reshape(b, 1, d), sc.reshape(b, 1, d),
      gt2.reshape(b, 1, d), wa, wg, cw, cb.reshape(1, f), wd)


def _key_to_float(key):
    bits = jnp.where(key < 0, jnp.int32(-2 ** 31) - key, key)
    return lax.bitcast_convert_type(bits, F32)


def _bf16_pieces(x):
    out, r = [], float(x)
    for _ in range(3):
        p = float(np.asarray(r, dtype=ml_dtypes.bfloat16).astype(np.float32))
        out.append(p)
        r -= p
    return tuple(out)


def _dsa_kernel(q_ref, qi_ref, wi_ref, ka_ref, vat_ref, ki_ref, gq_ref, o_ref,
                s_ref, b_ref, p_ref, m_ref, acc_ref, *, tq, tk, top_k, scale, slope_pieces):
    qt = pl.program_id(1)
    n_groups = ka_ref.shape[1]
    group = q_ref.shape[2] // HEAD_DIM // n_groups
    nch = ((qt + 1) * tq + tk - 1) // tk
    k_off = lax.broadcasted_iota(I32, (tk, tq), 0)
    q_pos = lax.broadcasted_iota(I32, (tk, tq), 1) + qt * tq
    kf = float(top_k)

    qi_t = qi_ref[0].T
    zpad = jnp.zeros((LANES - IDX_DIM, tq), F32)
    wqs = [jnp.concatenate([qi_t[hh * IDX_DIM:(hh + 1) * IDX_DIM], zpad], axis=0).astype(BF16)
           for hh in range(IDX_HEADS)]
    w_t = wi_ref[0].T[IDX_DIM:IDX_DIM + IDX_HEADS] * (IDX_HEADS ** -0.5 * IDX_DIM ** -0.5)

    def score_chunk(j, carry):
        kc = ki_ref[0, 0, j]
        sc = jnp.zeros((tk, tq), F32)
        for hh in range(IDX_HEADS):
            rel = jnp.maximum(jnp.dot(kc, wqs[hh], preferred_element_type=F32), 0.0)
            sc = sc + rel * w_t[hh:hh + 1]
        sc = jnp.where(k_off + j * tk <= q_pos, sc, -jnp.inf)
        s_ref[j] = sc
        b_ref[j] = sc.astype(BF16)
        return carry

    lax.fori_loop(0, nch, score_chunk, 0)

    def count(pred):
        ways = 8 if tk % (8 * SUBLANES) == 0 else 1

        def body(j, c):
            x = jnp.where(pred(s_ref[j], j), 1.0, 0.0).reshape(ways, tk // (ways * SUBLANES), SUBLANES, tq)
            for i in range(x.shape[1]):
                c = c + x[:, i]
            return c
        c = lax.fori_loop(0, nch, body, jnp.zeros((ways, SUBLANES, tq), F32))
        return jnp.sum(jnp.sum(c, axis=0), axis=0, keepdims=True)

    def count_coarse(c16):
        rows = 2 * SUBLANES
        ways = 8 if tk % (8 * rows) == 0 else 1
        one, zero = jnp.ones((), BF16), jnp.zeros((), BF16)

        def body(j, c):
            x = jnp.where(b_ref[j] >= c16, one, zero).reshape(ways, tk // (ways * rows), rows, tq)
            for i in range(x.shape[1]):
                c = c + x[:, i]
            return c
        c = lax.fori_loop(0, nch, body, jnp.zeros((ways, rows, tq), BF16))
        return jnp.sum(jnp.sum(c.astype(F32), axis=0), axis=0, keepdims=True)

    def search(_):
        def coarse_step(_, lohi):
            lo, hi = lohi
            mid = (lo + hi) >> 1
            ge = count_coarse(_key_to_float(mid << 16).astype(BF16)) >= kf
            return jnp.where(ge, mid, lo), jnp.where(ge, hi, mid)

        lo16, hi16 = lax.fori_loop(0, 16, coarse_step, (jnp.full((1, tq), KEY_NEG_INF >> 16, I32),
                                                        jnp.full((1, tq), (KEY_POS_INF >> 16) + 1, I32)))

        def unsettled(lo, hi, clo):
            return (clo != kf) & (hi > lo + 1)

        def flag(lo, hi, clo, chi):
            return jnp.max(jnp.where(unsettled(lo, hi, clo) & (clo - chi > 2.0), 1.0, 0.0))

        def cond(st):
            return st[4] > 0.0

        def step(st):
            lo, hi, clo, chi, _ = st
            active = unsettled(lo, hi, clo) & (clo - chi > 2.0)
            mid = (lo >> 1) + (hi >> 1) + (lo & hi & 1)
            cmid = _key_to_float(mid)
            cnt = count(lambda s, j: s >= cmid)
            ge = cnt >= kf
            take_lo = active & ge
            take_hi = active & jnp.logical_not(ge)
            lo = jnp.where(take_lo, mid, lo)
            clo = jnp.where(take_lo, cnt, clo)
            hi = jnp.where(take_hi, mid, hi)
            chi = jnp.where(take_hi, cnt, chi)
            return lo, hi, clo, chi, flag(lo, hi, clo, chi)

        lo0 = jnp.maximum((lo16 - 1) << 16, KEY_NEG_INF)
        hi0 = hi16 << 16
        f_lo0 = _key_to_float(lo0)
        clo0 = count(lambda s, j: s >= f_lo0)
        chi0 = jnp.zeros((1, tq), F32)
        lo, hi, clo, chi, _ = lax.while_loop(cond, step, (lo0, hi0, clo0, chi0, flag(lo0, hi0, clo0, chi0)))
        f_lo, f_hi = _key_to_float(lo), _key_to_float(hi)

        two_left = unsettled(lo, hi, clo)

        def extremes(_):
            ways = 8 if tk % (8 * SUBLANES) == 0 else 1
            shape4 = (ways, tk // (ways * SUBLANES), SUBLANES, tq)

            def body(j, c):
                vmax, vmin = c
                s = s_ref[j]
                below = jnp.where(s < f_hi, s, -jnp.inf).reshape(shape4)
                above = jnp.where(s >= f_lo, s, jnp.inf).reshape(shape4)
                for i in range(shape4[1]):
                    vmax = jnp.maximum(vmax, below[:, i])
                    vmin = jnp.minimum(vmin, above[:, i])
                return vmax, vmin

            vmax, vmin = lax.fori_loop(0, nch, body, (jnp.full((ways, SUBLANES, tq), -jnp.inf, F32),
                                                      jnp.full((ways, SUBLANES, tq), jnp.inf, F32)))
            return (jnp.max(jnp.max(vmax, axis=0), axis=0, keepdims=True),
                    jnp.min(jnp.min(vmin, axis=0), axis=0, keepdims=True))

        vmax, vmin = lax.cond(jnp.max(jnp.where(two_left, 1.0, 0.0)) > 0.0, extremes, lambda _: (f_lo, f_lo), 0)
        tau = jnp.where(two_left, vmax, f_lo)
        n_ge = jnp.where(two_left & (vmax != vmin), kf, clo)
        return tau, n_ge

    def no_search(_):
        return jnp.full((1, tq), -jnp.inf, F32), jnp.full((1, tq), kf, F32)

    tau, n_ge = lax.cond((qt + 1) * tq > top_k, search, no_search, 0)
    tau_sel = jnp.maximum(tau, F32_LOWEST)

    tie = (n_ge > kf) & (tau > -jnp.inf)
    any_tie = jnp.max(jnp.where(tie, 1.0, 0.0)) > 0.0

    def mask_plain(_):
        def body(j, carry):
            s_ref[j] = jnp.where(s_ref[j] >= tau_sel, 0.0, NEG_BIG)
            return carry
        lax.fori_loop(0, nch, body, 0)
        return 0

    def mask_ties(_):
        need = kf - count(lambda s, j: s > tau)

        def step(_, lohi):
            lo, hi = lohi
            mid = (lo + hi) >> 1
            ge = count(lambda s, j: (s == tau) & (k_off + j * tk <= mid)) >= need
            return jnp.where(ge, lo, mid), jnp.where(ge, mid, hi)

        steps = max(1, int(math.ceil(math.log2(s_ref.shape[0] * tk + 1))))
        lo0 = jnp.full((1, tq), -1, I32)
        hi0 = jnp.full((1, tq), 1, I32) * (nch * tk - 1)
        _, hi = lax.fori_loop(0, steps, step, (lo0, hi0))
        jstar = jnp.where(tie, hi, jnp.int32(2 ** 30))

        def body(j, carry):
            s = s_ref[j]
            eq_bias = jnp.where(k_off + j * tk <= jstar, 0.0, NEG_BIG)
            s_ref[j] = jnp.where(s > tau_sel, 0.0, jnp.where(s == tau_sel, eq_bias, NEG_BIG))
            return carry
        lax.fori_loop(0, nch, body, 0)
        return 0

    lax.cond(any_tie, mask_ties, mask_plain, 0)

    gq = gq_ref[...]
    q_t = q_ref[0].T
    r8 = lax.broadcasted_iota(I32, (AUG_ROWS, tq), 0)
    hb = m_ref.shape[0]
    for h0 in range(0, n_groups * group, hb):
        heads = list(range(h0, h0 + hb))
        ws = []
        for hq in heads:
            hi, mid, lo = slope_pieces[hq]
            rows = (POS_SPLIT * hi, POS_SPLIT * mid, POS_SPLIT * lo, hi, mid, lo)
            aug = jnp.zeros((AUG_ROWS, tq), F32)
            for i, val in enumerate(rows):
                aug = jnp.where(r8 == i, val, aug)
            ws.append(_query_weights(q_t[hq * HEAD_DIM:(hq + 1) * HEAD_DIM], gq, scale, aug))
        m_ref[...] = jnp.full(m_ref.shape, -jnp.inf, F32)
        acc_ref[...] = jnp.zeros(acc_ref.shape, F32)

        def qk(j, slot, heads=heads, ws=ws):
            for i, hq in enumerate(heads):
                p_ref[slot, i] = jnp.dot(ka_ref[0, hq // group, j], ws[i], preferred_element_type=F32) + s_ref[j]

        def process(j, slot, last, heads=heads):
            for i, hq in enumerate(heads):
                _softmax_step_t(p_ref[slot, i], m_ref, acc_ref, vat_ref[0, hq // group, j], i)

        _pipelined_chunks(nch, qk, process)
        outs = []
        for i in range(hb):
            acc = acc_ref[i]
            outs.append(acc[:HEAD_DIM] / acc[HEAD_DIM:HEAD_DIM + 1])
        o_ref[0, :, h0 * HEAD_DIM:(h0 + hb) * HEAD_DIM] = jnp.concatenate(outs, axis=0).T.astype(o_ref.dtype)


def _dsa_attention(proj, ka, vat, ki, g_q, n_q_heads, qi_col0, wi_col0, tq, tk, top_k):
    b, l, _ = proj.shape
    dh = HEAD_DIM
    n_groups, nk = ka.shape[1], ka.shape[2]
    qw = n_q_heads * dh
    iw = IDX_HEADS * IDX_DIM
    assert qi_col0 % iw == 0 and wi_col0 % LANES == IDX_DIM
    slope_pieces = tuple(_bf16_pieces(2.0 ** (-8.0 * (i + 1) / n_q_heads) * LOG2E) for i in range(n_q_heads))
    hb = math.gcd(8, n_q_heads)
    return pl.pallas_call(
        functools.partial(_dsa_kernel, tq=tq, tk=tk, top_k=top_k, scale=dh ** -0.5, slope_pieces=slope_pieces),
        grid=(b, l // tq),
        in_specs=[
            pl.BlockSpec((1, tq, qw), lambda i, j: (i, j, 0)),
            pl.BlockSpec((1, tq, iw), lambda i, j: (i, j, qi_col0 // iw)),
            pl.BlockSpec((1, tq, LANES), lambda i, j: (i, j, wi_col0 // LANES)),
            _resident((1, n_groups, nk, tk, LANES), lambda i, j: (i, 0, 0, 0, 0)),
            _resident((1, n_groups, nk, VA_ROWS, tk), lambda i, j: (i, 0, 0, 0, 0)),
            _resident((1, 1, nk, tk, LANES), lambda i, j: (i, 0, 0, 0, 0)),
            pl.BlockSpec((dh, 1), lambda i, j: (0, 0)),
        ],
        out_specs=pl.BlockSpec((1, tq, qw), lambda i, j: (i, j, 0)),
        out_shape=jax.ShapeDtypeStruct((b, l, qw), BF16),
        scratch_shapes=[
            pltpu.VMEM((nk, tk, tq), F32),
            pltpu.VMEM((nk, tk, tq), BF16),
            pltpu.VMEM((2, hb, tk, tq), F32),
            pltpu.VMEM((hb, 1, tq), F32),
            pltpu.VMEM((hb, VA_ROWS, tq), F32),
        ],
        compiler_params=_params("arbitrary", "arbitrary"),
        name="dsa_attn",
    )(proj, proj, proj, ka, vat, ki, g_q.reshape(dh, 1))


def _pad_cols(w, mult=LANES):
    n = w.shape[1]
    return jnp.pad(w, ((0, 0), (0, _round_up(n, mult) - n)))


def kernel(x, c, w_ada, b_ada, g_norm_mix, g_norm_ffn, fox_w_in, fox_b_f, fox_g_q, fox_g_k, fox_w_out,
           dsa_w_in, dsa_g_q, dsa_g_k, dsa_g_kidx, dsa_w_out, ffn_w_up, ffn_conv_w, ffn_conv_b, ffn_w_down):
    b, l, d = x.shape
    dh = HEAD_DIM
    n_heads = d // dh
    d_ff = ffn_w_down.shape[1]
    top_k = min(TOPK_MAX, l // 4)
    tq = min(Q_TILE, l)
    tk = min(K_TILE, l)

    mod = _adaln(c, w_ada, b_ada)

    def ffn(i, xx, o, w_out):
        sh2, sc2, gt2 = mod[i, :, 3 * d:4 * d], mod[i, :, 4 * d:5 * d], mod[i, :, 5 * d:6 * d]
        gt1 = mod[i, :, 2 * d:3 * d]
        wup = ffn_w_up[i].astype(BF16)
        return _post_attention_ffn(xx, o, w_out.astype(BF16), gt1, g_norm_ffn[i], sh2, sc2, gt2,
                                   wup[:, :d_ff], wup[:, d_ff:], ffn_conv_w[i], ffn_conv_b[i],
                                   ffn_w_down[i].astype(BF16))

    proj = _inproj(x, g_norm_mix[0], mod[0, :, 0:d], mod[0, :, d:2 * d], _pad_cols(fox_w_in[0]).astype(BF16))
    f_t = proj[..., 3 * d:3 * d + n_heads].transpose(0, 2, 1)
    cum = _cumgate(f_t, fox_b_f[0]).reshape(b, n_heads, l // tk, 1, tk)
    ka = _kaug_prep(proj, fox_g_k[0], d, n_heads, tk, "cum", cum)
    vat = _vat_prep(proj, 2 * d, n_heads, tk)
    o = _fox_attention(proj, ka, vat, fox_g_q[0], min(FOX_Q_TILE, l), tk)
    x = ffn(0, x, o, fox_w_out[0])

    g_kv = DSA_KV_HEADS
    kvw = g_kv * dh
    proj = _inproj(x, g_norm_mix[1], mod[1, :, 0:d], mod[1, :, d:2 * d], _pad_cols(dsa_w_in[0]).astype(BF16))
    o0, o1, o2 = d, d + kvw, d + 2 * kvw
    o3 = o2 + IDX_HEADS * IDX_DIM
    o4 = o3 + IDX_DIM
    ka = _kaug_prep(proj, dsa_g_k[0], o0, g_kv, tk, "pos")
    vat = _vat_prep(proj, o1, g_kv, tk)
    ki = _kaug_prep(proj, dsa_g_kidx[0], o3, 1, tk, "none")
    o = _dsa_attention(proj, ka, vat, ki, dsa_g_q[0], n_heads, o2, o4, tq, tk, top_k)
    x = ffn(1, x, o, dsa_w_out[0])
    return x
```

```python
import functools
import math

import jax
import jax.numpy as jnp
import ml_dtypes
import numpy as np
from jax import lax
from jax.experimental import pallas as pl
from jax.experimental.pallas import tpu as pltpu

F32 = jnp.float32
BF16 = jnp.bfloat16
I32 = jnp.int32

HEAD_DIM = 64
DSA_KV_HEADS = 4
IDX_HEADS = 8
IDX_DIM = 64
TOPK_MAX = 256
EPS = 1e-6
LOG2E = 1.4426950408889634
LANES = 128
SUBLANES = 8
VMEM_LIMIT = 56 * 1024 * 1024
NEG_BIG = -1e30
F32_LOWEST = -3.4028234663852886e38
KEY_NEG_INF = -0x7F800000
KEY_POS_INF = 0x7F800000
Q_TILE = 256
FOX_Q_TILE = 512
K_TILE = 512
AUG_ROWS = SUBLANES
VA_ROWS = 80
POS_SPLIT = 256


def _params(*sem):
    return pltpu.CompilerParams(dimension_semantics=sem, vmem_limit_bytes=VMEM_LIMIT)


def _resident(shape, index_map):
    return pl.BlockSpec(shape, index_map, pipeline_mode=pl.Buffered(1))


def _round_up(n, m):
    return (n + m - 1) // m * m


def _adaln_kernel(c_ref, w_ref, b_ref, o_ref):
    c = c_ref[...]
    ca = c * jax.nn.sigmoid(c)
    o_ref[0] = jnp.dot(ca, w_ref[0], preferred_element_type=F32) + b_ref[0]


def _adaln(c, w_ada, b_ada):
    depth, d, n = w_ada.shape
    b = c.shape[0]
    rows = _round_up(b, 8)
    c_pad = jnp.pad(c, ((0, rows - b), (0, 0)))
    tn = 1536
    out = pl.pallas_call(
        _adaln_kernel,
        grid=(depth, n // tn),
        in_specs=[
            pl.BlockSpec((rows, d), lambda i, j: (0, 0)),
            pl.BlockSpec((1, d, tn), lambda i, j: (i, 0, j)),
            pl.BlockSpec((1, 1, tn), lambda i, j: (i, 0, j)),
        ],
        out_specs=pl.BlockSpec((1, rows, tn), lambda i, j: (i, 0, j)),
        out_shape=jax.ShapeDtypeStruct((depth, rows, n), F32),
        compiler_params=_params("arbitrary", "arbitrary"),
        name="adaln",
    )(c_pad, w_ada, b_ada.reshape(depth, 1, n))
    return out[:, :b]


def _modulated(x, g, sh, sc):
    ms = jnp.mean(x * x, axis=-1, keepdims=True)
    return x * lax.rsqrt(ms + EPS) * (g * (1.0 + sc)) + sh


def _inproj_kernel(x_ref, g_ref, sh_ref, sc_ref, w_ref, o_ref):
    h = _modulated(x_ref[0], g_ref[...], sh_ref[0], sc_ref[0])
    o_ref[0] = jnp.dot(h.astype(BF16), w_ref[...], preferred_element_type=F32)


def _inproj(x, g, sh, sc, w_bf16, tm=512):
    b, l, d = x.shape
    n = w_bf16.shape[1]
    tm = min(tm, l)
    return pl.pallas_call(
        _inproj_kernel,
        grid=(b, l // tm),
        in_specs=[
            pl.BlockSpec((1, tm, d), lambda i, j: (i, j, 0)),
            pl.BlockSpec((1, d), lambda i, j: (0, 0)),
            pl.BlockSpec((1, 1, d), lambda i, j: (i, 0, 0)),
            pl.BlockSpec((1, 1, d), lambda i, j: (i, 0, 0)),
            _resident((d, n), lambda i, j: (0, 0)),
        ],
        out_specs=pl.BlockSpec((1, tm, n), lambda i, j: (i, j, 0)),
        out_shape=jax.ShapeDtypeStruct((b, l, n), F32),
        compiler_params=_params("arbitrary", "arbitrary"),
        name="inproj",
    )(x, g.reshape(1, d), sh.reshape(b, 1, d), sc.reshape(b, 1, d), w_bf16)


def _split3(x):
    hi = x.astype(BF16)
    r = x - hi.astype(F32)
    mid = r.astype(BF16)
    lo = (r - mid.astype(F32)).astype(BF16)
    return hi, mid, lo


def _cumgate_kernel(f_ref, b_ref, o_ref, *, cw):
    x = f_ref[0] + b_ref[...]
    lf = jnp.minimum(x, 0.0) - jnp.log1p(jnp.exp(-jnp.abs(x)))
    h, l = lf.shape
    row = lax.broadcasted_iota(I32, (cw, cw), 0)
    col = lax.broadcasted_iota(I32, (cw, cw), 1)
    tri = jnp.where(row <= col, 1.0, 0.0).astype(BF16)
    carry = jnp.zeros((h, 1), F32)
    for c in range(l // cw):
        hi, mid, lo = _split3(lf[:, c * cw:(c + 1) * cw])
        cs = (jnp.dot(hi, tri, preferred_element_type=F32)
              + jnp.dot(mid, tri, preferred_element_type=F32)
              + jnp.dot(lo, tri, preferred_element_type=F32)) + carry
        o_ref[0, :, c * cw:(c + 1) * cw] = cs * LOG2E
        carry = cs[:, cw - 1:cw]


def _cumgate(f_t, b_f):
    b, h, l = f_t.shape
    cw = min(256, l)
    return pl.pallas_call(
        functools.partial(_cumgate_kernel, cw=cw),
        grid=(b,),
        in_specs=[
            pl.BlockSpec((1, h, l), lambda i: (i, 0, 0)),
            pl.BlockSpec((h, 1), lambda i: (0, 0)),
        ],
        out_specs=pl.BlockSpec((1, h, l), lambda i: (i, 0, 0)),
        out_shape=jax.ShapeDtypeStruct((b, h, l), F32),
        compiler_params=_params("arbitrary"),
        name="cumgate",
    )(f_t, b_f.reshape(h, 1))


def _norm_rows(xh, g, extra_scale=1.0):
    ms = jnp.mean(xh * xh, axis=0, keepdims=True)
    return xh * lax.rsqrt(ms + EPS) * (g * extra_scale)


def _kaug_prep_kernel(*refs, n_heads, mode, tk):
    if mode == "cum":
        x_ref, g_ref, c_ref, o_ref = refs
    else:
        x_ref, g_ref, o_ref = refs
    j = pl.program_id(1)
    g = g_ref[...]
    r8 = lax.broadcasted_iota(I32, (AUG_ROWS, tk), 0)
    pad = jnp.zeros((LANES - HEAD_DIM - AUG_ROWS, tk), F32)
    if mode == "pos":
        kpos = lax.broadcasted_iota(I32, (AUG_ROWS, tk), 1) + j * tk
        a = (kpos // POS_SPLIT).astype(F32)
        bb = (kpos % POS_SPLIT).astype(F32)
        aug_pos = jnp.where(r8 < 3, a, jnp.where(r8 < 6, bb, 0.0))
    for hp in range((n_heads + 1) // 2):
        xt = x_ref[0, :, hp * LANES:(hp + 1) * LANES].T
        for hh in range(min(2, n_heads - 2 * hp)):
            h = 2 * hp + hh
            kn = _norm_rows(xt[hh * HEAD_DIM:(hh + 1) * HEAD_DIM], g)
            if mode == "cum":
                hi, mid, lo = _split3(c_ref[0, h, 0])
                aug = jnp.where(r8 == 0, hi.astype(F32),
                                jnp.where(r8 == 1, mid.astype(F32), jnp.where(r8 == 2, lo.astype(F32), 0.0)))
            elif mode == "pos":
                aug = aug_pos
            else:
                aug = jnp.zeros((AUG_ROWS, tk), F32)
            t = jnp.concatenate([kn, aug, pad], axis=0)
            o_ref[0, h, 0] = t.T.astype(o_ref.dtype)


def _kaug_prep(proj, g, col0, n_heads, tk, mode, cum=None):
    b, l, _ = proj.shape
    w = max(LANES, n_heads * HEAD_DIM)
    assert col0 % w == 0
    nk = l // tk
    in_specs = [
        pl.BlockSpec((1, tk, w), lambda i, j: (i, j, col0 // w)),
        pl.BlockSpec((HEAD_DIM, 1), lambda i, j: (0, 0)),
    ]
    args = [proj, g.reshape(HEAD_DIM, 1)]
    if mode == "cum":
        in_specs.append(pl.BlockSpec((1, n_heads, 1, 1, tk), lambda i, j: (i, 0, j, 0, 0)))
        args.append(cum)
    return pl.pallas_call(
        functools.partial(_kaug_prep_kernel, n_heads=n_heads, mode=mode, tk=tk),
        grid=(b, nk),
        in_specs=in_specs,
        out_specs=pl.BlockSpec((1, n_heads, 1, tk, LANES), lambda i, j: (i, 0, j, 0, 0)),
        out_shape=jax.ShapeDtypeStruct((b, n_heads, nk, tk, LANES), BF16),
        compiler_params=_params("arbitrary", "arbitrary"),
        name="kaug_prep",
    )(*args)


def _vat_prep_kernel(x_ref, o_ref, *, n_heads, tk):
    r16 = lax.broadcasted_iota(I32, (VA_ROWS - HEAD_DIM, tk), 0)
    ones_rows = jnp.where(r16 == 0, 1.0, 0.0).astype(o_ref.dtype)
    for hp in range(n_heads // 2):
        xt = x_ref[0, :, hp * LANES:(hp + 1) * LANES].T
        for hh in range(2):
            h = 2 * hp + hh
            o_ref[0, h, 0, :HEAD_DIM] = xt[hh * HEAD_DIM:(hh + 1) * HEAD_DIM].astype(o_ref.dtype)
            o_ref[0, h, 0, HEAD_DIM:] = ones_rows


def _vat_prep(proj, col0, n_heads, tk):
    b, l, _ = proj.shape
    w = n_heads * HEAD_DIM
    assert col0 % w == 0 and n_heads % 2 == 0
    nk = l // tk
    return pl.pallas_call(
        functools.partial(_vat_prep_kernel, n_heads=n_heads, tk=tk),
        grid=(b, nk),
        in_specs=[pl.BlockSpec((1, tk, w), lambda i, j: (i, j, col0 // w))],
        out_specs=pl.BlockSpec((1, n_heads, 1, VA_ROWS, tk), lambda i, j: (i, 0, j, 0, 0)),
        out_shape=jax.ShapeDtypeStruct((b, n_heads, nk, VA_ROWS, tk), BF16),
        compiler_params=_params("arbitrary", "arbitrary"),
        name="vat_prep",
    )(proj)


def _col_reduce(x, op):
    rows, n = x.shape
    ways = 8 if rows % (8 * SUBLANES) == 0 else 1
    x = op(x.reshape(ways, rows // (ways * SUBLANES), SUBLANES, n), axis=1)
    return op(op(x, axis=0), axis=0, keepdims=True)


def _softmax_step_t(s, m_ref, acc_ref, vat, idx):
    m_prev = m_ref[idx]
    m_new = jnp.maximum(m_prev, _col_reduce(s, jnp.max))
    p = jnp.exp2(s - m_new).astype(BF16)
    alpha = jnp.exp2(m_prev - m_new)
    acc_ref[idx] = alpha * acc_ref[idx] + jnp.dot(vat, p, preferred_element_type=F32)
    m_ref[idx] = m_new


def _pipelined_chunks(nch, qk, process):
    pairs = (nch - 1) // 2
    rest = nch - 2 * pairs
    qk(0, 0)

    def body(jj, carry):
        j = 2 * jj
        qk(j + 1, 1)
        process(j, 0, False)
        qk(j + 2, 0)
        process(j + 1, 1, False)
        return carry

    lax.fori_loop(0, pairs, body, 0)

    @pl.when(rest == 1)
    def _():
        process(2 * pairs, 0, True)

    @pl.when(rest == 2)
    def _():
        qk(2 * pairs + 1, 1)
        process(2 * pairs, 0, False)
        process(2 * pairs + 1, 1, True)


def _query_weights(q_t, g, scale, aug):
    qn = _norm_rows(q_t, g, scale * LOG2E)
    pad = jnp.zeros((LANES - HEAD_DIM - AUG_ROWS, q_t.shape[1]), F32)
    return jnp.concatenate([qn, aug, pad], axis=0).astype(BF16)


def _fox_kernel(q_ref, ka_ref, vat_ref, gq_ref, o_ref, s_ref, m_ref, acc_ref, *, tq, tk, hb, scale):
    qi = pl.program_id(2)
    gq = gq_ref[...]
    r8 = lax.broadcasted_iota(I32, (AUG_ROWS, tq), 0)
    aug = jnp.where(r8 < 3, -1.0, 0.0)
    q_t = q_ref[0].T
    ws = [_query_weights(q_t[h * HEAD_DIM:(h + 1) * HEAD_DIM], gq, scale, aug) for h in range(hb)]
    m_ref[...] = jnp.full(m_ref.shape, -jnp.inf, F32)
    acc_ref[...] = jnp.zeros(acc_ref.shape, F32)
    nch = ((qi + 1) * tq + tk - 1) // tk

    def qk(j, slot):
        for h in range(hb):
            s_ref[slot, h] = jnp.dot(ka_ref[0, h, j], ws[h], preferred_element_type=F32)

    def process(j, slot, last):
        for h in range(hb):
            s = s_ref[slot, h]
            if last:
                k_pos = lax.broadcasted_iota(I32, (tk, tq), 0) + j * tk
                q_pos = lax.broadcasted_iota(I32, (tk, tq), 1) + qi * tq
                s = jnp.where(k_pos <= q_pos, s, -jnp.inf)
            _softmax_step_t(s, m_ref, acc_ref, vat_ref[0, h, j], h)

    _pipelined_chunks(nch, qk, process)
    outs = []
    for h in range(hb):
        acc = acc_ref[h]
        outs.append(acc[:HEAD_DIM] / acc[HEAD_DIM:HEAD_DIM + 1])
    o_ref[0] = jnp.concatenate(outs, axis=0).T.astype(o_ref.dtype)


def _fox_attention(proj, ka, vat, g_q, tq, tk, hb=8):
    b, l, _ = proj.shape
    h, nk = ka.shape[1], ka.shape[2]
    dh = HEAD_DIM
    hb = math.gcd(hb, h)
    return pl.pallas_call(
        functools.partial(_fox_kernel, tq=tq, tk=tk, hb=hb, scale=dh ** -0.5),
        grid=(b, h // hb, l // tq),
        in_specs=[
            pl.BlockSpec((1, tq, hb * dh), lambda i, j, k: (i, k, j)),
            _resident((1, hb, nk, tk, LANES), lambda i, j, k: (i, j, 0, 0, 0)),
            _resident((1, hb, nk, VA_ROWS, tk), lambda i, j, k: (i, j, 0, 0, 0)),
            pl.BlockSpec((dh, 1), lambda i, j, k: (0, 0)),
        ],
        out_specs=pl.BlockSpec((1, tq, hb * dh), lambda i, j, k: (i, k, j)),
        out_shape=jax.ShapeDtypeStruct((b, l, h * dh), BF16),
        scratch_shapes=[
            pltpu.VMEM((2, hb, tk, tq), F32),
            pltpu.VMEM((hb, 1, tq), F32),
            pltpu.VMEM((hb, VA_ROWS, tq), F32),
        ],
        compiler_params=_params("arbitrary", "arbitrary", "arbitrary"),
        name="fox_attn",
    )(proj, ka, vat, g_q.reshape(dh, 1))


def _ffn_kernel(x_ref, o_ref, wo_ref, gt1_ref, g_ref, sh_ref, sc_ref, gt2_ref,
                wa_ref, wg_ref, cw_ref, cb_ref, wd_ref, out_ref, abuf_ref, *, tm, fc):
    i = pl.program_id(1)
    y = jnp.dot(o_ref[0], wo_ref[...], preferred_element_type=F32)
    x1 = x_ref[0] + gt1_ref[0] * y
    h = _modulated(x1, g_ref[...], sh_ref[0], sc_ref[0]).astype(BF16)

    @pl.when(i == 0)
    def _():
        abuf_ref[0:8, :] = jnp.zeros((8, abuf_ref.shape[1]), F32)

    @pl.when(i > 0)
    def _():
        abuf_ref[0:8, :] = abuf_ref[tm:tm + 8, :]

    f = abuf_ref.shape[1]
    y2 = None
    for c in range(f // fc):
        sl = slice(c * fc, (c + 1) * fc)
        a = jnp.dot(h, wa_ref[:, sl], preferred_element_type=F32)
        gate = jnp.dot(h, wg_ref[:, sl], preferred_element_type=F32)
        abuf_ref[8:8 + tm, sl] = a
        a1 = abuf_ref[7:7 + tm, sl]
        a2 = abuf_ref[6:6 + tm, sl]
        ac = cw_ref[0:1, sl] * a2 + cw_ref[1:2, sl] * a1 + cw_ref[2:3, sl] * a + cb_ref[:, sl]
        act = (ac * jax.nn.sigmoid(ac) * gate).astype(BF16)
        part = jnp.dot(act, wd_ref[sl, :], preferred_element_type=F32)
        y2 = part if y2 is None else y2 + part
    out_ref[0] = x1 + gt2_ref[0] * y2


def _post_attention_ffn(x, o, wo, gt1, g, sh, sc, gt2, wa, wg, cw, cb, wd, tm=512):
    b, l, d = x.shape
    f = wa.shape[1]
    tm = min(tm, l)
    row = lambda i, j: (i, j, 0)
    per_batch = lambda i, j: (i, 0, 0)
    const = lambda i, j: (0, 0)
    return pl.pallas_call(
        functools.partial(_ffn_kernel, tm=tm, fc=math.gcd(f, 256)),
        grid=(b, l // tm),
        in_specs=[
            pl.BlockSpec((1, tm, d), row),
            pl.BlockSpec((1, tm, d), row),
            _resident((d, d), const),
            pl.BlockSpec((1, 1, d), per_batch),
            pl.BlockSpec((1, d), const),
            pl.BlockSpec((1, 1, d), per_batch),
            pl.BlockSpec((1, 1, d), per_batch),
            pl.BlockSpec((1, 1, d), per_batch),
            _resident((d, f), const),
            _resident((d, f), const),
            pl.BlockSpec((3, f), const),
            pl.BlockSpec((1, f), const),
            _resident((f, d), const),
        ],
        out_specs=pl.BlockSpec((1, tm, d), row),
        out_shape=jax.ShapeDtypeStruct((b, l, d), F32),
        scratch_shapes=[pltpu.VMEM((tm + 8, f), F32)],
        compiler_params=_params("arbitrary", "arbitrary"),
        name="post_attn_ffn",
    )(x, o, wo, gt1.reshape(b, 1, d), g.reshape(1, d), sh.reshape(b, 1, d), sc.reshape(b, 1, d),
      gt2.reshape(b, 1, d), wa, wg, cw, cb.reshape(1, f), wd)


def _key_to_float(key):
    bits = jnp.where(key < 0, jnp.int32(-2 ** 31) - key, key)
    return lax.bitcast_convert_type(bits, F32)


def _bf16_pieces(x):
    out, r = [], float(x)
    for _ in range(3):
        p = float(np.asarray(r, dtype=ml_dtypes.bfloat16).astype(np.float32))
        out.append(p)
        r -= p
    return tuple(out)


def _dsa_kernel(q_ref, qi_ref, wi_ref, ka_ref, vat_ref, ki_ref, gq_ref, o_ref,
                s_ref, b_ref, p_ref, m_ref, acc_ref, *, tq, tk, top_k, scale, slope_pieces):
    qt = pl.program_id(1)
    n_groups = ka_ref.shape[1]
    group = q_ref.shape[2] // HEAD_DIM // n_groups
    nch = ((qt + 1) * tq + tk - 1) // tk
    k_off = lax.broadcasted_iota(I32, (tk, tq), 0)
    q_pos = lax.broadcasted_iota(I32, (tk, tq), 1) + qt * tq
    kf = float(top_k)

    qi_t = qi_ref[0].T
    zpad = jnp.zeros((LANES - IDX_DIM, tq), F32)
    wqs = [jnp.concatenate([qi_t[hh * IDX_DIM:(hh + 1) * IDX_DIM], zpad], axis=0).astype(BF16)
           for hh in range(IDX_HEADS)]
    w_t = wi_ref[0].T[IDX_DIM:IDX_DIM + IDX_HEADS] * (IDX_HEADS ** -0.5 * IDX_DIM ** -0.5)

    def score_chunk(j, carry):
        kc = ki_ref[0, 0, j]
        sc = jnp.zeros((tk, tq), F32)
        for hh in range(IDX_HEADS):
            rel = jnp.maximum(jnp.dot(kc, wqs[hh], preferred_element_type=F32), 0.0)
            sc = sc + rel * w_t[hh:hh + 1]
        sc = jnp.where(k_off + j * tk <= q_pos, sc, -jnp.inf)
        s_ref[j] = sc
        b_ref[j] = sc.astype(BF16)
        return carry

    lax.fori_loop(0, nch, score_chunk, 0)

    def count(pred):
        ways = 2 if tk % (8 * SUBLANES) == 0 else 1

        def body(j, c):
            x = jnp.where(pred(s_ref[j], j), 1.0, 0.0).reshape(ways, tk // (ways * SUBLANES), SUBLANES, tq)
            for i in range(x.shape[1]):
                c = c + x[:, i]
            return c
        c = lax.fori_loop(0, nch, body, jnp.zeros((ways, SUBLANES, tq), F32))
        return jnp.sum(jnp.sum(c, axis=0), axis=0, keepdims=True)

    def count_coarse(c16):
        rows = 2 * SUBLANES
        ways = 2 if tk % (8 * rows) == 0 else 1
        one, zero = jnp.ones((), BF16), jnp.zeros((), BF16)

        def body(j, c):
            x = jnp.where(b_ref[j] >= c16, one, zero).reshape(ways, tk // (ways * rows), rows, tq)
            for i in range(x.shape[1]):
                c = c + x[:, i]
            return c
        c = lax.fori_loop(0, nch, body, jnp.zeros((ways, rows, tq), BF16))
        return jnp.sum(jnp.sum(c.astype(F32), axis=0), axis=0, keepdims=True)

    def search(_):
        def coarse_step(_, lohi):
            lo, hi = lohi
            mid = (lo + hi) >> 1
            ge = count_coarse(_key_to_float(mid << 16).astype(BF16)) >= kf
            return jnp.where(ge, mid, lo), jnp.where(ge, hi, mid)

        lo16, hi16 = lax.fori_loop(0, 16, coarse_step, (jnp.full((1, tq), KEY_NEG_INF >> 16, I32),
                                                        jnp.full((1, tq), (KEY_POS_INF >> 16) + 1, I32)))

        def unsettled(lo, hi, clo):
            return (clo != kf) & (hi > lo + 1)

        def flag(lo, hi, clo, chi):
            return jnp.max(jnp.where(unsettled(lo, hi, clo) & (clo - chi > 2.0), 1.0, 0.0))

        def cond(st):
            return st[4] > 0.0

        def step(st):
            lo, hi, clo, chi, _ = st
            active = unsettled(lo, hi, clo) & (clo - chi > 2.0)
            mid = (lo >> 1) + (hi >> 1) + (lo & hi & 1)
            cmid = _key_to_float(mid)
            cnt = count(lambda s, j: s >= cmid)
            ge = cnt >= kf
            take_lo = active & ge
            take_hi = active & jnp.logical_not(ge)
            lo = jnp.where(take_lo, mid, lo)
            clo = jnp.where(take_lo, cnt, clo)
            hi = jnp.where(take_hi, mid, hi)
            chi = jnp.where(take_hi, cnt, chi)
            return lo, hi, clo, chi, flag(lo, hi, clo, chi)

        lo0 = jnp.maximum((lo16 - 1) << 16, KEY_NEG_INF)
        hi0 = hi16 << 16
        clo0 = jnp.full((1, tq), 2.0 ** 24, F32)
        chi0 = jnp.zeros((1, tq), F32)
        lo, hi, clo, chi, _ = lax.while_loop(cond, step, (lo0, hi0, clo0, chi0, flag(lo0, hi0, clo0, chi0)))
        f_lo, f_hi = _key_to_float(lo), _key_to_float(hi)
        never_moved = lo == lo0
        clo = lax.cond(jnp.max(jnp.where(never_moved, 1.0, 0.0)) > 0.0,
                       lambda _: jnp.where(never_moved, count(lambda s, j: s >= f_lo), clo),
                       lambda _: clo, 0)

        two_left = unsettled(lo, hi, clo)

        def extremes(_):
            ways = 2 if tk % (8 * SUBLANES) == 0 else 1
            shape4 = (ways, tk // (ways * SUBLANES), SUBLANES, tq)

            def body(j, c):
                vmax, vmin = c
                s = s_ref[j]
                below = jnp.where(s < f_hi, s, -jnp.inf).reshape(shape4)
                above = jnp.where(s >= f_lo, s, jnp.inf).reshape(shape4)
                for i in range(shape4[1]):
                    vmax = jnp.maximum(vmax, below[:, i])
                    vmin = jnp.minimum(vmin, above[:, i])
                return vmax, vmin

            vmax, vmin = lax.fori_loop(0, nch, body, (jnp.full((ways, SUBLANES, tq), -jnp.inf, F32),
                                                      jnp.full((ways, SUBLANES, tq), jnp.inf, F32)))
            return (jnp.max(jnp.max(vmax, axis=0), axis=0, keepdims=True),
                    jnp.min(jnp.min(vmin, axis=0), axis=0, keepdims=True))

        vmax, vmin = lax.cond(jnp.max(jnp.where(two_left, 1.0, 0.0)) > 0.0, extremes, lambda _: (f_lo, f_lo), 0)
        tau = jnp.where(two_left, vmax, f_lo)
        n_ge = jnp.where(two_left & (vmax != vmin), kf, clo)
        return tau, n_ge

    def no_search(_):
        return jnp.full((1, tq), -jnp.inf, F32), jnp.full((1, tq), kf, F32)

    tau, n_ge = lax.cond((qt + 1) * tq > top_k, search, no_search, 0)
    tau_sel = jnp.maximum(tau, F32_LOWEST)

    tie = (n_ge > kf) & (tau > -jnp.inf)
    any_tie = jnp.max(jnp.where(tie, 1.0, 0.0)) > 0.0

    def mask_plain(_):
        def body(j, carry):
            s_ref[j] = jnp.where(s_ref[j] >= tau_sel, 0.0, NEG_BIG)
            return carry
        lax.fori_loop(0, nch, body, 0)
        return 0

    def mask_ties(_):
        need = kf - count(lambda s, j: s > tau)

        def step(_, lohi):
            lo, hi = lohi
            mid = (lo + hi) >> 1
            ge = count(lambda s, j: (s == tau) & (k_off + j * tk <= mid)) >= need
            return jnp.where(ge, lo, mid), jnp.where(ge, mid, hi)

        steps = max(1, int(math.ceil(math.log2(s_ref.shape[0] * tk + 1))))
        lo0 = jnp.full((1, tq), -1, I32)
        hi0 = jnp.full((1, tq), 1, I32) * (nch * tk - 1)
        _, hi = lax.fori_loop(0, steps, step, (lo0, hi0))
        jstar = jnp.where(tie, hi, jnp.int32(2 ** 30))

        def body(j, carry):
            s = s_ref[j]
            eq_bias = jnp.where(k_off + j * tk <= jstar, 0.0, NEG_BIG)
            s_ref[j] = jnp.where(s > tau_sel, 0.0, jnp.where(s == tau_sel, eq_bias, NEG_BIG))
            return carry
        lax.fori_loop(0, nch, body, 0)
        return 0

    lax.cond(any_tie, mask_ties, mask_plain, 0)

    gq = gq_ref[...]
    q_t = q_ref[0].T
    r8 = lax.broadcasted_iota(I32, (AUG_ROWS, tq), 0)
    hb = m_ref.shape[0]
    for h0 in range(0, n_groups * group, hb):
        heads = list(range(h0, h0 + hb))
        ws = []
        for hq in heads:
            hi, mid, lo = slope_pieces[hq]
            rows = (POS_SPLIT * hi, POS_SPLIT * mid, POS_SPLIT * lo, hi, mid, lo)
            aug = jnp.zeros((AUG_ROWS, tq), F32)
            for i, val in enumerate(rows):
                aug = jnp.where(r8 == i, val, aug)
            ws.append(_query_weights(q_t[hq * HEAD_DIM:(hq + 1) * HEAD_DIM], gq, scale, aug))
        m_ref[...] = jnp.full(m_ref.shape, -jnp.inf, F32)
        acc_ref[...] = jnp.zeros(acc_ref.shape, F32)

        def qk(j, slot, heads=heads, ws=ws):
            for i, hq in enumerate(heads):
                p_ref[slot, i] = jnp.dot(ka_ref[0, hq // group, j], ws[i], preferred_element_type=F32) + s_ref[j]

        def process(j, slot, last, heads=heads):
            for i, hq in enumerate(heads):
                _softmax_step_t(p_ref[slot, i], m_ref, acc_ref, vat_ref[0, hq // group, j], i)

        _pipelined_chunks(nch, qk, process)
        outs = []
        for i in range(hb):
            acc = acc_ref[i]
            outs.append(acc[:HEAD_DIM] / acc[HEAD_DIM:HEAD_DIM + 1])
        o_ref[0, :, h0 * HEAD_DIM:(h0 + hb) * HEAD_DIM] = jnp.concatenate(outs, axis=0).T.astype(o_ref.dtype)


def _dsa_attention(proj, ka, vat, ki, g_q, n_q_heads, qi_col0, wi_col0, tq, tk, top_k):
    b, l, _ = proj.shape
    dh = HEAD_DIM
    n_groups, nk = ka.shape[1], ka.shape[2]
    qw = n_q_heads * dh
    iw = IDX_HEADS * IDX_DIM
    assert qi_col0 % iw == 0 and wi_col0 % LANES == IDX_DIM
    slope_pieces = tuple(_bf16_pieces(2.0 ** (-8.0 * (i + 1) / n_q_heads) * LOG2E) for i in range(n_q_heads))
    hb = math.gcd(8, n_q_heads)
    return pl.pallas_call(
        functools.partial(_dsa_kernel, tq=tq, tk=tk, top_k=top_k, scale=dh ** -0.5, slope_pieces=slope_pieces),
        grid=(b, l // tq),
        in_specs=[
            pl.BlockSpec((1, tq, qw), lambda i, j: (i, j, 0)),
            pl.BlockSpec((1, tq, iw), lambda i, j: (i, j, qi_col0 // iw)),
            pl.BlockSpec((1, tq, LANES), lambda i, j: (i, j, wi_col0 // LANES)),
            _resident((1, n_groups, nk, tk, LANES), lambda i, j: (i, 0, 0, 0, 0)),
            _resident((1, n_groups, nk, VA_ROWS, tk), lambda i, j: (i, 0, 0, 0, 0)),
            _resident((1, 1, nk, tk, LANES), lambda i, j: (i, 0, 0, 0, 0)),
            pl.BlockSpec((dh, 1), lambda i, j: (0, 0)),
        ],
        out_specs=pl.BlockSpec((1, tq, qw), lambda i, j: (i, j, 0)),
        out_shape=jax.ShapeDtypeStruct((b, l, qw), BF16),
        scratch_shapes=[
            pltpu.VMEM((nk, tk, tq), F32),
            pltpu.VMEM((nk, tk, tq), BF16),
            pltpu.VMEM((2, hb, tk, tq), F32),
            pltpu.VMEM((hb, 1, tq), F32),
            pltpu.VMEM((hb, VA_ROWS, tq), F32),
        ],
        compiler_params=_params("arbitrary", "arbitrary"),
        name="dsa_attn",
    )(proj, proj, proj, ka, vat, ki, g_q.reshape(dh, 1))


def _pad_cols(w, mult=LANES):
    n = w.shape[1]
    return jnp.pad(w, ((0, 0), (0, _round_up(n, mult) - n)))


def kernel(x, c, w_ada, b_ada, g_norm_mix, g_norm_ffn, fox_w_in, fox_b_f, fox_g_q, fox_g_k, fox_w_out,
           dsa_w_in, dsa_g_q, dsa_g_k, dsa_g_kidx, dsa_w_out, ffn_w_up, ffn_conv_w, ffn_conv_b, ffn_w_down):
    b, l, d = x.shape
    dh = HEAD_DIM
    n_heads = d // dh
    d_ff = ffn_w_down.shape[1]
    top_k = min(TOPK_MAX, l // 4)
    tq = min(Q_TILE, l)
    tk = min(K_TILE, l)

    mod = _adaln(c, w_ada, b_ada)

    def ffn(i, xx, o, w_out):
        sh2, sc2, gt2 = mod[i, :, 3 * d:4 * d], mod[i, :, 4 * d:5 * d], mod[i, :, 5 * d:6 * d]
        gt1 = mod[i, :, 2 * d:3 * d]
        wup = ffn_w_up[i].astype(BF16)
        return _post_attention_ffn(xx, o, w_out.astype(BF16), gt1, g_norm_ffn[i], sh2, sc2, gt2,
                                   wup[:, :d_ff], wup[:, d_ff:], ffn_conv_w[i], ffn_conv_b[i],
                                   ffn_w_down[i].astype(BF16))

    proj = _inproj(x, g_norm_mix[0], mod[0, :, 0:d], mod[0, :, d:2 * d], _pad_cols(fox_w_in[0]).astype(BF16))
    f_t = proj[..., 3 * d:3 * d + n_heads].transpose(0, 2, 1)
    cum = _cumgate(f_t, fox_b_f[0]).reshape(b, n_heads, l // tk, 1, tk)
    ka = _kaug_prep(proj, fox_g_k[0], d, n_heads, tk, "cum", cum)
    vat = _vat_prep(proj, 2 * d, n_heads, tk)
    o = _fox_attention(proj, ka, vat, fox_g_q[0], min(FOX_Q_TILE, l), tk)
    x = ffn(0, x, o, fox_w_out[0])

    g_kv = DSA_KV_HEADS
    kvw = g_kv * dh
    proj = _inproj(x, g_norm_mix[1], mod[1, :, 0:d], mod[1, :, d:2 * d], _pad_cols(dsa_w_in[0]).astype(BF16))
    o0, o1, o2 = d, d + kvw, d + 2 * kvw
    o3 = o2 + IDX_HEADS * IDX_DIM
    o4 = o3 + IDX_DIM
    ka = _kaug_prep(proj, dsa_g_k[0], o0, g_kv, tk, "pos")
    vat = _vat_prep(proj, o1, g_kv, tk)
    ki = _kaug_prep(proj, dsa_g_kidx[0], o3, 1, tk, "none")
    o = _dsa_attention(proj, ka, vat, ki, dsa_g_q[0], n_heads, o2, o4, tq, tk, top_k)
    x = ffn(1, x, o, dsa_w_out[0])
    return x
```

```python
import functools
import math

import jax
import jax.numpy as jnp
import ml_dtypes
import numpy as np
from jax import lax
from jax.experimental import pallas as pl
from jax.experimental.pallas import tpu as pltpu

F32 = jnp.float32
BF16 = jnp.bfloat16
I32 = jnp.int32

HEAD_DIM = 64
DSA_KV_HEADS = 4
IDX_HEADS = 8
IDX_DIM = 64
TOPK_MAX = 256
EPS = 1e-6
LOG2E = 1.4426950408889634
LANES = 128
SUBLANES = 8
VMEM_LIMIT = 56 * 1024 * 1024
NEG_BIG = -1e30
F32_LOWEST = -3.4028234663852886e38
KEY_NEG_INF = -0x7F800000
KEY_POS_INF = 0x7F800000
Q_TILE = 256
FOX_Q_TILE = 512
K_TILE = 512
AUG_ROWS = SUBLANES
VA_ROWS = 80
POS_SPLIT = 256


def _params(*sem):
    return pltpu.CompilerParams(dimension_semantics=sem, vmem_limit_bytes=VMEM_LIMIT)


def _resident(shape, index_map):
    return pl.BlockSpec(shape, index_map, pipeline_mode=pl.Buffered(1))


def _round_up(n, m):
    return (n + m - 1) // m * m


def _adaln_kernel(c_ref, w_ref, b_ref, o_ref):
    c = c_ref[...]
    ca = c * jax.nn.sigmoid(c)
    o_ref[0] = jnp.dot(ca, w_ref[0], preferred_element_type=F32) + b_ref[0]


def _adaln(c, w_ada, b_ada):
    depth, d, n = w_ada.shape
    b = c.shape[0]
    rows = _round_up(b, 8)
    c_pad = jnp.pad(c, ((0, rows - b), (0, 0)))
    tn = 1536
    out = pl.pallas_call(
        _adaln_kernel,
        grid=(depth, n // tn),
        in_specs=[
            pl.BlockSpec((rows, d), lambda i, j: (0, 0)),
            pl.BlockSpec((1, d, tn), lambda i, j: (i, 0, j)),
            pl.BlockSpec((1, 1, tn), lambda i, j: (i, 0, j)),
        ],
        out_specs=pl.BlockSpec((1, rows, tn), lambda i, j: (i, 0, j)),
        out_shape=jax.ShapeDtypeStruct((depth, rows, n), F32),
        compiler_params=_params("arbitrary", "arbitrary"),
        name="adaln",
    )(c_pad, w_ada, b_ada.reshape(depth, 1, n))
    return out[:, :b]


def _modulated(x, g, sh, sc):
    ms = jnp.mean(x * x, axis=-1, keepdims=True)
    return x * lax.rsqrt(ms + EPS) * (g * (1.0 + sc)) + sh


def _inproj_kernel(x_ref, g_ref, sh_ref, sc_ref, w_ref, o_ref):
    h = _modulated(x_ref[0], g_ref[...], sh_ref[0], sc_ref[0])
    o_ref[0] = jnp.dot(h.astype(BF16), w_ref[...], preferred_element_type=F32)


def _inproj(x, g, sh, sc, w_bf16, tm=512):
    b, l, d = x.shape
    n = w_bf16.shape[1]
    tm = min(tm, l)
    return pl.pallas_call(
        _inproj_kernel,
        grid=(b, l // tm),
        in_specs=[
            pl.BlockSpec((1, tm, d), lambda i, j: (i, j, 0)),
            pl.BlockSpec((1, d), lambda i, j: (0, 0)),
            pl.BlockSpec((1, 1, d), lambda i, j: (i, 0, 0)),
            pl.BlockSpec((1, 1, d), lambda i, j: (i, 0, 0)),
            _resident((d, n), lambda i, j: (0, 0)),
        ],
        out_specs=pl.BlockSpec((1, tm, n), lambda i, j: (i, j, 0)),
        out_shape=jax.ShapeDtypeStruct((b, l, n), F32),
        compiler_params=_params("arbitrary", "arbitrary"),
        name="inproj",
    )(x, g.reshape(1, d), sh.reshape(b, 1, d), sc.reshape(b, 1, d), w_bf16)


def _split3(x):
    hi = x.astype(BF16)
    r = x - hi.astype(F32)
    mid = r.astype(BF16)
    lo = (r - mid.astype(F32)).astype(BF16)
    return hi, mid, lo


def _cumgate_kernel(f_ref, b_ref, o_ref, *, cw):
    x = f_ref[0] + b_ref[...]
    lf = jnp.minimum(x, 0.0) - jnp.log1p(jnp.exp(-jnp.abs(x)))
    h, l = lf.shape
    row = lax.broadcasted_iota(I32, (cw, cw), 0)
    col = lax.broadcasted_iota(I32, (cw, cw), 1)
    tri = jnp.where(row <= col, 1.0, 0.0).astype(BF16)
    carry = jnp.zeros((h, 1), F32)
    for c in range(l // cw):
        hi, mid, lo = _split3(lf[:, c * cw:(c + 1) * cw])
        cs = (jnp.dot(hi, tri, preferred_element_type=F32)
              + jnp.dot(mid, tri, preferred_element_type=F32)
              + jnp.dot(lo, tri, preferred_element_type=F32)) + carry
        o_ref[0, :, c * cw:(c + 1) * cw] = cs * LOG2E
        carry = cs[:, cw - 1:cw]


def _cumgate(f_t, b_f):
    b, h, l = f_t.shape
    cw = min(256, l)
    return pl.pallas_call(
        functools.partial(_cumgate_kernel, cw=cw),
        grid=(b,),
        in_specs=[
            pl.BlockSpec((1, h, l), lambda i: (i, 0, 0)),
            pl.BlockSpec((h, 1), lambda i: (0, 0)),
        ],
        out_specs=pl.BlockSpec((1, h, l), lambda i: (i, 0, 0)),
        out_shape=jax.ShapeDtypeStruct((b, h, l), F32),
        compiler_params=_params("arbitrary"),
        name="cumgate",
    )(f_t, b_f.reshape(h, 1))


def _norm_rows(xh, g, extra_scale=1.0):
    ms = jnp.mean(xh * xh, axis=0, keepdims=True)
    return xh * lax.rsqrt(ms + EPS) * (g * extra_scale)


def _kaug_prep_kernel(*refs, n_heads, mode, tk):
    if mode == "cum":
        x_ref, g_ref, c_ref, o_ref = refs
    else:
        x_ref, g_ref, o_ref = refs
    j = pl.program_id(1)
    g = g_ref[...]
    r8 = lax.broadcasted_iota(I32, (AUG_ROWS, tk), 0)
    pad = jnp.zeros((LANES - HEAD_DIM - AUG_ROWS, tk), F32)
    if mode == "pos":
        kpos = lax.broadcasted_iota(I32, (AUG_ROWS, tk), 1) + j * tk
        a = (kpos // POS_SPLIT).astype(F32)
        bb = (kpos % POS_SPLIT).astype(F32)
        aug_pos = jnp.where(r8 < 3, a, jnp.where(r8 < 6, bb, 0.0))
    for hp in range((n_heads + 1) // 2):
        xt = x_ref[0, :, hp * LANES:(hp + 1) * LANES].T
        for hh in range(min(2, n_heads - 2 * hp)):
            h = 2 * hp + hh
            kn = _norm_rows(xt[hh * HEAD_DIM:(hh + 1) * HEAD_DIM], g)
            if mode == "cum":
                hi, mid, lo = _split3(c_ref[0, h, 0])
                aug = jnp.where(r8 == 0, hi.astype(F32),
                                jnp.where(r8 == 1, mid.astype(F32), jnp.where(r8 == 2, lo.astype(F32), 0.0)))
            elif mode == "pos":
                aug = aug_pos
            else:
                aug = jnp.zeros((AUG_ROWS, tk), F32)
            t = jnp.concatenate([kn, aug, pad], axis=0)
            o_ref[0, h, 0] = t.T.astype(o_ref.dtype)


def _kaug_prep(proj, g, col0, n_heads, tk, mode, cum=None):
    b, l, _ = proj.shape
    w = max(LANES, n_heads * HEAD_DIM)
    assert col0 % w == 0
    nk = l // tk
    in_specs = [
        pl.BlockSpec((1, tk, w), lambda i, j: (i, j, col0 // w)),
        pl.BlockSpec((HEAD_DIM, 1), lambda i, j: (0, 0)),
    ]
    args = [proj, g.reshape(HEAD_DIM, 1)]
    if mode == "cum":
        in_specs.append(pl.BlockSpec((1, n_heads, 1, 1, tk), lambda i, j: (i, 0, j, 0, 0)))
        args.append(cum)
    return pl.pallas_call(
        functools.partial(_kaug_prep_kernel, n_heads=n_heads, mode=mode, tk=tk),
        grid=(b, nk),
        in_specs=in_specs,
        out_specs=pl.BlockSpec((1, n_heads, 1, tk, LANES), lambda i, j: (i, 0, j, 0, 0)),
        out_shape=jax.ShapeDtypeStruct((b, n_heads, nk, tk, LANES), BF16),
        compiler_params=_params("arbitrary", "arbitrary"),
        name="kaug_prep",
    )(*args)


def _vat_prep_kernel(x_ref, o_ref, *, n_heads, tk):
    r16 = lax.broadcasted_iota(I32, (VA_ROWS - HEAD_DIM, tk), 0)
    ones_rows = jnp.where(r16 == 0, 1.0, 0.0).astype(o_ref.dtype)
    for hp in range(n_heads // 2):
        xt = x_ref[0, :, hp * LANES:(hp + 1) * LANES].T
        for hh in range(2):
            h = 2 * hp + hh
            o_ref[0, h, 0, :HEAD_DIM] = xt[hh * HEAD_DIM:(hh + 1) * HEAD_DIM].astype(o_ref.dtype)
            o_ref[0, h, 0, HEAD_DIM:] = ones_rows


def _vat_prep(proj, col0, n_heads, tk):
    b, l, _ = proj.shape
    w = n_heads * HEAD_DIM
    assert col0 % w == 0 and n_heads % 2 == 0
    nk = l // tk
    return pl.pallas_call(
        functools.partial(_vat_prep_kernel, n_heads=n_heads, tk=tk),
        grid=(b, nk),
        in_specs=[pl.BlockSpec((1, tk, w), lambda i, j: (i, j, col0 // w))],
        out_specs=pl.BlockSpec((1, n_heads, 1, VA_ROWS, tk), lambda i, j: (i, 0, j, 0, 0)),
        out_shape=jax.ShapeDtypeStruct((b, n_heads, nk, VA_ROWS, tk), BF16),
        compiler_params=_params("arbitrary", "arbitrary"),
        name="vat_prep",
    )(proj)


def _col_reduce(x, op):
    rows, n = x.shape
    ways = 8 if rows % (8 * SUBLANES) == 0 else 1
    x = op(x.reshape(ways, rows // (ways * SUBLANES), SUBLANES, n), axis=1)
    return op(op(x, axis=0), axis=0, keepdims=True)


def _softmax_step_t(s, m_ref, acc_ref, vat, idx):
    m_prev = m_ref[idx]
    m_new = jnp.maximum(m_prev, _col_reduce(s, jnp.max))
    p = jnp.exp2(s - m_new).astype(BF16)
    alpha = jnp.exp2(m_prev - m_new)
    acc_ref[idx] = alpha * acc_ref[idx] + jnp.dot(vat, p, preferred_element_type=F32)
    m_ref[idx] = m_new


def _pipelined_chunks(nch, qk, process):
    pairs = (nch - 1) // 2
    rest = nch - 2 * pairs
    qk(0, 0)

    def body(jj, carry):
        j = 2 * jj
        qk(j + 1, 1)
        process(j, 0, False)
        qk(j + 2, 0)
        process(j + 1, 1, False)
        return carry

    lax.fori_loop(0, pairs, body, 0)

    @pl.when(rest == 1)
    def _():
        process(2 * pairs, 0, True)

    @pl.when(rest == 2)
    def _():
        qk(2 * pairs + 1, 1)
        process(2 * pairs, 0, False)
        process(2 * pairs + 1, 1, True)


def _query_weights(q_t, g, scale, aug):
    qn = _norm_rows(q_t, g, scale * LOG2E)
    pad = jnp.zeros((LANES - HEAD_DIM - AUG_ROWS, q_t.shape[1]), F32)
    return jnp.concatenate([qn, aug, pad], axis=0).astype(BF16)


def _fox_kernel(q_ref, ka_ref, vat_ref, gq_ref, o_ref, s_ref, m_ref, acc_ref, *, tq, tk, hb, scale):
    qi = pl.program_id(2)
    gq = gq_ref[...]
    r8 = lax.broadcasted_iota(I32, (AUG_ROWS, tq), 0)
    aug = jnp.where(r8 < 3, -1.0, 0.0)
    q_t = q_ref[0].T
    ws = [_query_weights(q_t[h * HEAD_DIM:(h + 1) * HEAD_DIM], gq, scale, aug) for h in range(hb)]
    m_ref[...] = jnp.full(m_ref.shape, -jnp.inf, F32)
    acc_ref[...] = jnp.zeros(acc_ref.shape, F32)
    nch = ((qi + 1) * tq + tk - 1) // tk

    def qk(j, slot):
        for h in range(hb):
            s_ref[slot, h] = jnp.dot(ka_ref[0, h, j], ws[h], preferred_element_type=F32)

    def process(j, slot, last):
        for h in range(hb):
            s = s_ref[slot, h]
            if last:
                k_pos = lax.broadcasted_iota(I32, (tk, tq), 0) + j * tk
                q_pos = lax.broadcasted_iota(I32, (tk, tq), 1) + qi * tq
                s = jnp.where(k_pos <= q_pos, s, -jnp.inf)
            _softmax_step_t(s, m_ref, acc_ref, vat_ref[0, h, j], h)

    _pipelined_chunks(nch, qk, process)
    outs = []
    for h in range(hb):
        acc = acc_ref[h]
        outs.append(acc[:HEAD_DIM] / acc[HEAD_DIM:HEAD_DIM + 1])
    o_ref[0] = jnp.concatenate(outs, axis=0).T.astype(o_ref.dtype)


def _fox_attention(proj, ka, vat, g_q, tq, tk, hb=8):
    b, l, _ = proj.shape
    h, nk = ka.shape[1], ka.shape[2]
    dh = HEAD_DIM
    hb = math.gcd(hb, h)
    return pl.pallas_call(
        functools.partial(_fox_kernel, tq=tq, tk=tk, hb=hb, scale=dh ** -0.5),
        grid=(b, h // hb, l // tq),
        in_specs=[
            pl.BlockSpec((1, tq, hb * dh), lambda i, j, k: (i, k, j)),
            _resident((1, hb, nk, tk, LANES), lambda i, j, k: (i, j, 0, 0, 0)),
            _resident((1, hb, nk, VA_ROWS, tk), lambda i, j, k: (i, j, 0, 0, 0)),
            pl.BlockSpec((dh, 1), lambda i, j, k: (0, 0)),
        ],
        out_specs=pl.BlockSpec((1, tq, hb * dh), lambda i, j, k: (i, k, j)),
        out_shape=jax.ShapeDtypeStruct((b, l, h * dh), BF16),
        scratch_shapes=[
            pltpu.VMEM((2, hb, tk, tq), F32),
            pltpu.VMEM((hb, 1, tq), F32),
            pltpu.VMEM((hb, VA_ROWS, tq), F32),
        ],
        compiler_params=_params("arbitrary", "arbitrary", "arbitrary"),
        name="fox_attn",
    )(proj, ka, vat, g_q.reshape(dh, 1))


def _ffn_kernel(x_ref, o_ref, wo_ref, gt1_ref, g_ref, sh_ref, sc_ref, gt2_ref,
                wa_ref, wg_ref, cw_ref, cb_ref, wd_ref, out_ref, abuf_ref, *, tm, fc):
    i = pl.program_id(1)
    y = jnp.dot(o_ref[0], wo_ref[...], preferred_element_type=F32)
    x1 = x_ref[0] + gt1_ref[0] * y
    h = _modulated(x1, g_ref[...], sh_ref[0], sc_ref[0]).astype(BF16)

    @pl.when(i == 0)
    def _():
        abuf_ref[0:8, :] = jnp.zeros((8, abuf_ref.shape[1]), F32)

    @pl.when(i > 0)
    def _():
        abuf_ref[0:8, :] = abuf_ref[tm:tm + 8, :]

    f = abuf_ref.shape[1]
    y2 = None
    for c in range(f // fc):
        sl = slice(c * fc, (c + 1) * fc)
        a = jnp.dot(h, wa_ref[:, sl], preferred_element_type=F32)
        gate = jnp.dot(h, wg_ref[:, sl], preferred_element_type=F32)
        abuf_ref[8:8 + tm, sl] = a
        a1 = abuf_ref[7:7 + tm, sl]
        a2 = abuf_ref[6:6 + tm, sl]
        ac = cw_ref[0:1, sl] * a2 + cw_ref[1:2, sl] * a1 + cw_ref[2:3, sl] * a + cb_ref[:, sl]
        act = (ac * jax.nn.sigmoid(ac) * gate).astype(BF16)
        part = jnp.dot(act, wd_ref[sl, :], preferred_element_type=F32)
        y2 = part if y2 is None else y2 + part
    out_ref[0] = x1 + gt2_ref[0] * y2


def _post_attention_ffn(x, o, wo, gt1, g, sh, sc, gt2, wa, wg, cw, cb, wd, tm=512):
    b, l, d = x.shape
    f = wa.shape[1]
    tm = min(tm, l)
    row = lambda i, j: (i, j, 0)
    per_batch = lambda i, j: (i, 0, 0)
    const = lambda i, j: (0, 0)
    return pl.pallas_call(
        functools.partial(_ffn_kernel, tm=tm, fc=math.gcd(f, 256)),
        grid=(b, l // tm),
        in_specs=[
            pl.BlockSpec((1, tm, d), row),
            pl.BlockSpec((1, tm, d), row),
            _resident((d, d), const),
            pl.BlockSpec((1, 1, d), per_batch),
            pl.BlockSpec((1, d), const),
            pl.BlockSpec((1, 1, d), per_batch),
            pl.BlockSpec((1, 1, d), per_batch),
            pl.BlockSpec((1, 1, d), per_batch),
            _resident((d, f), const),
            _resident((d, f), const),
            pl.BlockSpec((3, f), const),
            pl.BlockSpec((1, f), const),
            _resident((f, d), const),
        ],
        out_specs=pl.BlockSpec((1, tm, d), row),
        out_shape=jax.ShapeDtypeStruct((b, l, d), F32),
        scratch_shapes=[pltpu.VMEM((tm + 8, f), F32)],
        compiler_params=_params("arbitrary", "arbitrary"),
        name="post_attn_ffn",
    )(x, o, wo, gt1.reshape(b, 1, d), g.reshape(1, d), sh.reshape(b, 1, d), sc.reshape(b, 1, d),
      gt2.reshape(b, 1, d), wa, wg, cw, cb.reshape(1, f), wd)


def _key_to_float(key):
    bits = jnp.where(key < 0, jnp.int32(-2 ** 31) - key, key)
    return lax.bitcast_convert_type(bits, F32)


def _bf16_pieces(x):
    out, r = [], float(x)
    for _ in range(3):
        p = float(np.asarray(r, dtype=ml_dtypes.bfloat16).astype(np.float32))
        out.append(p)
        r -= p
    return tuple(out)


def _dsa_kernel(q_ref, qi_ref, wi_ref, ka_ref, vat_ref, ki_ref, gq_ref, o_ref,
                s_ref, b_ref, p_ref, m_ref, acc_ref, *, tq, tk, top_k, scale, slope_pieces):
    qt = pl.program_id(1)
    n_groups = ka_ref.shape[1]
    group = q_ref.shape[2] // HEAD_DIM // n_groups
    nch = ((qt + 1) * tq + tk - 1) // tk
    k_off = lax.broadcasted_iota(I32, (tk, tq), 0)
    q_pos = lax.broadcasted_iota(I32, (tk, tq), 1) + qt * tq
    kf = float(top_k)

    qi_t = qi_ref[0].T
    zpad = jnp.zeros((LANES - IDX_DIM, tq), F32)
    wqs = [jnp.concatenate([qi_t[hh * IDX_DIM:(hh + 1) * IDX_DIM], zpad], axis=0).astype(BF16)
           for hh in range(IDX_HEADS)]
    w_t = wi_ref[0].T[IDX_DIM:IDX_DIM + IDX_HEADS] * (IDX_HEADS ** -0.5 * IDX_DIM ** -0.5)

    def score_chunk(j, carry):
        kc = ki_ref[0, 0, j]
        sc = jnp.zeros((tk, tq), F32)
        for hh in range(IDX_HEADS):
            rel = jnp.maximum(jnp.dot(kc, wqs[hh], preferred_element_type=F32), 0.0)
            sc = sc + rel * w_t[hh:hh + 1]
        sc = jnp.where(k_off + j * tk <= q_pos, sc, -jnp.inf)
        s_ref[j] = sc
        b_ref[j] = sc.astype(BF16)
        return carry

    lax.fori_loop(0, nch, score_chunk, 0)

    def count(pred):
        ways = 2 if tk % (8 * SUBLANES) == 0 else 1

        def body(j, c):
            x = jnp.where(pred(s_ref[j], j), 1.0, 0.0).reshape(ways, tk // (ways * SUBLANES), SUBLANES, tq)
            for i in range(x.shape[1]):
                c = c + x[:, i]
            return c
        c = lax.fori_loop(0, nch, body, jnp.zeros((ways, SUBLANES, tq), F32))
        return jnp.sum(jnp.sum(c, axis=0), axis=0, keepdims=True)

    def count_coarse(c16):
        rows = 2 * SUBLANES
        ways = 2 if tk % (8 * rows) == 0 else 1
        one, zero = jnp.ones((), BF16), jnp.zeros((), BF16)

        def body(j, c):
            x = jnp.where(b_ref[j] >= c16, one, zero).reshape(ways, tk // (ways * rows), rows, tq)
            for i in range(x.shape[1]):
                c = c + x[:, i]
            return c
        c = lax.fori_loop(0, nch, body, jnp.zeros((ways, rows, tq), BF16))
        return jnp.sum(jnp.sum(c.astype(F32), axis=0), axis=0, keepdims=True)

    def search(_):
        def coarse_step(_, lohi):
            lo, hi = lohi
            mid = (lo + hi) >> 1
            ge = count_coarse(_key_to_float(mid << 16).astype(BF16)) >= kf
            return jnp.where(ge, mid, lo), jnp.where(ge, hi, mid)

        lo16, hi16 = lax.fori_loop(0, 16, coarse_step, (jnp.full((1, tq), KEY_NEG_INF >> 16, I32),
                                                        jnp.full((1, tq), (KEY_POS_INF >> 16) + 1, I32)))

        def unsettled(lo, hi, clo):
            return (clo != kf) & (hi > lo + 1)

        def flag(lo, hi, clo, chi):
            return jnp.max(jnp.where(unsettled(lo, hi, clo) & (clo - chi > 2.0), 1.0, 0.0))

        def cond(st):
            return st[4] > 0.0

        def step(st):
            lo, hi, clo, chi, _ = st
            active = unsettled(lo, hi, clo) & (clo - chi > 2.0)
            mid = (lo >> 1) + (hi >> 1) + (lo & hi & 1)
            cmid = _key_to_float(mid)
            cnt = count(lambda s, j: s >= cmid)
            ge = cnt >= kf
            take_lo = active & ge
            take_hi = active & jnp.logical_not(ge)
            lo = jnp.where(take_lo, mid, lo)
            clo = jnp.where(take_lo, cnt, clo)
            hi = jnp.where(take_hi, mid, hi)
            chi = jnp.where(take_hi, cnt, chi)
            return lo, hi, clo, chi, flag(lo, hi, clo, chi)

        lo0 = jnp.maximum((lo16 - 1) << 16, KEY_NEG_INF)
        hi0 = hi16 << 16
        clo0 = jnp.full((1, tq), 2.0 ** 24, F32)
        chi0 = jnp.zeros((1, tq), F32)
        lo, hi, clo, chi, _ = lax.while_loop(cond, step, (lo0, hi0, clo0, chi0, flag(lo0, hi0, clo0, chi0)))
        f_lo, f_hi = _key_to_float(lo), _key_to_float(hi)
        never_moved = lo == lo0
        clo = lax.cond(jnp.max(jnp.where(never_moved, 1.0, 0.0)) > 0.0,
                       lambda _: jnp.where(never_moved, count(lambda s, j: s >= f_lo), clo),
                       lambda _: clo, 0)

        two_left = unsettled(lo, hi, clo)

        def extremes(_):
            ways = 2 if tk % (8 * SUBLANES) == 0 else 1
            shape4 = (ways, tk // (ways * SUBLANES), SUBLANES, tq)

            def body(j, c):
                vmax, vmin = c
                s = s_ref[j]
                below = jnp.where(s < f_hi, s, -jnp.inf).reshape(shape4)
                above = jnp.where(s >= f_lo, s, jnp.inf).reshape(shape4)
                for i in range(shape4[1]):
                    vmax = jnp.maximum(vmax, below[:, i])
                    vmin = jnp.minimum(vmin, above[:, i])
                return vmax, vmin

            vmax, vmin = lax.fori_loop(0, nch, body, (jnp.full((ways, SUBLANES, tq), -jnp.inf, F32),
                                                      jnp.full((ways, SUBLANES, tq), jnp.inf, F32)))
            return (jnp.max(jnp.max(vmax, axis=0), axis=0, keepdims=True),
                    jnp.min(jnp.min(vmin, axis=0), axis=0, keepdims=True))

        vmax, vmin = lax.cond(jnp.max(jnp.where(two_left, 1.0, 0.0)) > 0.0, extremes, lambda _: (f_lo, f_lo), 0)
        tau = jnp.where(two_left, vmax, f_lo)
        n_ge = jnp.where(two_left & (vmax != vmin), kf, clo)
        return tau, n_ge

    def no_search(_):
        return jnp.full((1, tq), -jnp.inf, F32), jnp.full((1, tq), kf, F32)

    tau, n_ge = lax.cond((qt + 1) * tq > top_k, search, no_search, 0)
    tau_sel = jnp.maximum(tau, F32_LOWEST)

    tie = (n_ge > kf) & (tau > -jnp.inf)
    any_tie = jnp.max(jnp.where(tie, 1.0, 0.0)) > 0.0

    def mask_plain(_):
        def body(j, carry):
            s_ref[j] = jnp.where(s_ref[j] >= tau_sel, 0.0, NEG_BIG)
            return carry
        lax.fori_loop(0, nch, body, 0)
        return 0

    def mask_ties(_):
        need = kf - count(lambda s, j: s > tau)

        def step(_, lohi):
            lo, hi = lohi
            mid = (lo + hi) >> 1
            ge = count(lambda s, j: (s == tau) & (k_off + j * tk <= mid)) >= need
            return jnp.where(ge, lo, mid), jnp.where(ge, mid, hi)

        steps = max(1, int(math.ceil(math.log2(s_ref.shape[0] * tk + 1))))
        lo0 = jnp.full((1, tq), -1, I32)
        hi0 = jnp.full((1, tq), 1, I32) * (nch * tk - 1)
        _, hi = lax.fori_loop(0, steps, step, (lo0, hi0))
        jstar = jnp.where(tie, hi, jnp.int32(2 ** 30))

        def body(j, carry):
            s = s_ref[j]
            eq_bias = jnp.where(k_off + j * tk <= jstar, 0.0, NEG_BIG)
            s_ref[j] = jnp.where(s > tau_sel, 0.0, jnp.where(s == tau_sel, eq_bias, NEG_BIG))
            return carry
        lax.fori_loop(0, nch, body, 0)
        return 0

    lax.cond(any_tie, mask_ties, mask_plain, 0)

    gq = gq_ref[...]
    q_t = q_ref[0].T
    r8 = lax.broadcasted_iota(I32, (AUG_ROWS, tq), 0)
    hb = m_ref.shape[0]
    for h0 in range(0, n_groups * group, hb):
        heads = list(range(h0, h0 + hb))
        ws = []
        for hq in heads:
            hi, mid, lo = slope_pieces[hq]
            rows = (POS_SPLIT * hi, POS_SPLIT * mid, POS_SPLIT * lo, hi, mid, lo)
            aug = jnp.zeros((AUG_ROWS, tq), F32)
            for i, val in enumerate(rows):
                aug = jnp.where(r8 == i, val, aug)
            ws.append(_query_weights(q_t[hq * HEAD_DIM:(hq + 1) * HEAD_DIM], gq, scale, aug))
        m_ref[...] = jnp.full(m_ref.shape, -jnp.inf, F32)
        acc_ref[...] = jnp.zeros(acc_ref.shape, F32)

        def qk(j, slot, heads=heads, ws=ws):
            for i, hq in enumerate(heads):
                p_ref[slot, i] = jnp.dot(ka_ref[0, hq // group, j], ws[i], preferred_element_type=F32) + s_ref[j]

        def process(j, slot, last, heads=heads):
            for i, hq in enumerate(heads):
                _softmax_step_t(p_ref[slot, i], m_ref, acc_ref, vat_ref[0, hq // group, j], i)

        _pipelined_chunks(nch, qk, process)
        outs = []
        for i in range(hb):
            acc = acc_ref[i]
            outs.append(acc[:HEAD_DIM] / acc[HEAD_DIM:HEAD_DIM + 1])
        o_ref[0, :, h0 * HEAD_DIM:(h0 + hb) * HEAD_DIM] = jnp.concatenate(outs, axis=0).T.astype(o_ref.dtype)


def _dsa_attention(proj, ka, vat, ki, g_q, n_q_heads, qi_col0, wi_col0, tq, tk, top_k):
    b, l, _ = proj.shape
    dh = HEAD_DIM
    n_groups, nk = ka.shape[1], ka.shape[2]
    qw = n_q_heads * dh
    iw = IDX_HEADS * IDX_DIM
    assert qi_col0 % iw == 0 and wi_col0 % LANES == IDX_DIM
    slope_pieces = tuple(_bf16_pieces(2.0 ** (-8.0 * (i + 1) / n_q_heads) * LOG2E) for i in range(n_q_heads))
    hb = math.gcd(16, n_q_heads)
    return pl.pallas_call(
        functools.partial(_dsa_kernel, tq=tq, tk=tk, top_k=top_k, scale=dh ** -0.5, slope_pieces=slope_pieces),
        grid=(b, l // tq),
        in_specs=[
            pl.BlockSpec((1, tq, qw), lambda i, j: (i, j, 0)),
            pl.BlockSpec((1, tq, iw), lambda i, j: (i, j, qi_col0 // iw)),
            pl.BlockSpec((1, tq, LANES), lambda i, j: (i, j, wi_col0 // LANES)),
            _resident((1, n_groups, nk, tk, LANES), lambda i, j: (i, 0, 0, 0, 0)),
            _resident((1, n_groups, nk, VA_ROWS, tk), lambda i, j: (i, 0, 0, 0, 0)),
            _resident((1, 1, nk, tk, LANES), lambda i, j: (i, 0, 0, 0, 0)),
            pl.BlockSpec((dh, 1), lambda i, j: (0, 0)),
        ],
        out_specs=pl.BlockSpec((1, tq, qw), lambda i, j: (i, j, 0)),
        out_shape=jax.ShapeDtypeStruct((b, l, qw), BF16),
        scratch_shapes=[
            pltpu.VMEM((nk, tk, tq), F32),
            pltpu.VMEM((nk, tk, tq), BF16),
            pltpu.VMEM((2, hb, tk, tq), F32),
            pltpu.VMEM((hb, 1, tq), F32),
            pltpu.VMEM((hb, VA_ROWS, tq), F32),
        ],
        compiler_params=_params("arbitrary", "arbitrary"),
        name="dsa_attn",
    )(proj, proj, proj, ka, vat, ki, g_q.reshape(dh, 1))


def _pad_cols(w, mult=LANES):
    n = w.shape[1]
    return jnp.pad(w, ((0, 0), (0, _round_up(n, mult) - n)))


def kernel(x, c, w_ada, b_ada, g_norm_mix, g_norm_ffn, fox_w_in, fox_b_f, fox_g_q, fox_g_k, fox_w_out,
           dsa_w_in, dsa_g_q, dsa_g_k, dsa_g_kidx, dsa_w_out, ffn_w_up, ffn_conv_w, ffn_conv_b, ffn_w_down):
    b, l, d = x.shape
    dh = HEAD_DIM
    n_heads = d // dh
    d_ff = ffn_w_down.shape[1]
    top_k = min(TOPK_MAX, l // 4)
    tq = min(Q_TILE, l)
    tk = min(K_TILE, l)

    mod = _adaln(c, w_ada, b_ada)

    def ffn(i, xx, o, w_out):
        sh2, sc2, gt2 = mod[i, :, 3 * d:4 * d], mod[i, :, 4 * d:5 * d], mod[i, :, 5 * d:6 * d]
        gt1 = mod[i, :, 2 * d:3 * d]
        wup = ffn_w_up[i].astype(BF16)
        return _post_attention_ffn(xx, o, w_out.astype(BF16), gt1, g_norm_ffn[i], sh2, sc2, gt2,
                                   wup[:, :d_ff], wup[:, d_ff:], ffn_conv_w[i], ffn_conv_b[i],
                                   ffn_w_down[i].astype(BF16))

    proj = _inproj(x, g_norm_mix[0], mod[0, :, 0:d], mod[0, :, d:2 * d], _pad_cols(fox_w_in[0]).astype(BF16))
    f_t = proj[..., 3 * d:3 * d + n_heads].transpose(0, 2, 1)
    cum = _cumgate(f_t, fox_b_f[0]).reshape(b, n_heads, l // tk, 1, tk)
    ka = _kaug_prep(proj, fox_g_k[0], d, n_heads, tk, "cum", cum)
    vat = _vat_prep(proj, 2 * d, n_heads, tk)
    o = _fox_attention(proj, ka, vat, fox_g_q[0], min(FOX_Q_TILE, l), tk)
    x = ffn(0, x, o, fox_w_out[0])

    g_kv = DSA_KV_HEADS
    kvw = g_kv * dh
    proj = _inproj(x, g_norm_mix[1], mod[1, :, 0:d], mod[1, :, d:2 * d], _pad_cols(dsa_w_in[0]).astype(BF16))
    o0, o1, o2 = d, d + kvw, d + 2 * kvw
    o3 = o2 + IDX_HEADS * IDX_DIM
    o4 = o3 + IDX_DIM
    ka = _kaug_prep(proj, dsa_g_k[0], o0, g_kv, tk, "pos")
    vat = _vat_prep(proj, o1, g_kv, tk)
    ki = _kaug_prep(proj, dsa_g_kidx[0], o3, 1, tk, "none")
    o = _dsa_attention(proj, ka, vat, ki, dsa_g_q[0], n_heads, o2, o4, tq, tk, top_k)
    x = ffn(1, x, o, dsa_w_out[0])
    return x
```

```python
import functools
import math

import jax
import jax.numpy as jnp
import ml_dtypes
import numpy as np
from jax import lax
from jax.experimental import pallas as pl
from jax.experimental.pallas import tpu as pltpu

F32 = jnp.float32
BF16 = jnp.bfloat16
I32 = jnp.int32

HEAD_DIM = 64
DSA_KV_HEADS = 4
IDX_HEADS = 8
IDX_DIM = 64
TOPK_MAX = 256
EPS = 1e-6
LOG2E = 1.4426950408889634
LANES = 128
SUBLANES = 8
VMEM_LIMIT = 56 * 1024 * 1024
NEG_BIG = -1e30
F32_LOWEST = -3.4028234663852886e38
KEY_NEG_INF = -0x7F800000
KEY_POS_INF = 0x7F800000
Q_TILE = 256
FOX_Q_TILE = 512
K_TILE = 512
AUG_ROWS = SUBLANES
VA_ROWS = 80
POS_SPLIT = 256


def _params(*sem):
    return pltpu.CompilerParams(dimension_semantics=sem, vmem_limit_bytes=VMEM_LIMIT)


def _resident(shape, index_map):
    return pl.BlockSpec(shape, index_map, pipeline_mode=pl.Buffered(1))


def _round_up(n, m):
    return (n + m - 1) // m * m


def _adaln_kernel(c_ref, w_ref, b_ref, o_ref):
    c = c_ref[...]
    ca = c * jax.nn.sigmoid(c)
    o_ref[0] = jnp.dot(ca, w_ref[0], preferred_element_type=F32) + b_ref[0]


def _adaln(c, w_ada, b_ada):
    depth, d, n = w_ada.shape
    b = c.shape[0]
    rows = _round_up(b, 8)
    c_pad = jnp.pad(c, ((0, rows - b), (0, 0)))
    tn = 1536
    out = pl.pallas_call(
        _adaln_kernel,
        grid=(depth, n // tn),
        in_specs=[
            pl.BlockSpec((rows, d), lambda i, j: (0, 0)),
            pl.BlockSpec((1, d, tn), lambda i, j: (i, 0, j)),
            pl.BlockSpec((1, 1, tn), lambda i, j: (i, 0, j)),
        ],
        out_specs=pl.BlockSpec((1, rows, tn), lambda i, j: (i, 0, j)),
        out_shape=jax.ShapeDtypeStruct((depth, rows, n), F32),
        compiler_params=_params("arbitrary", "arbitrary"),
        name="adaln",
    )(c_pad, w_ada, b_ada.reshape(depth, 1, n))
    return out[:, :b]


def _modulated(x, g, sh, sc):
    ms = jnp.mean(x * x, axis=-1, keepdims=True)
    return x * lax.rsqrt(ms + EPS) * (g * (1.0 + sc)) + sh


def _inproj_kernel(x_ref, g_ref, sh_ref, sc_ref, w_ref, o_ref):
    h = _modulated(x_ref[0], g_ref[...], sh_ref[0], sc_ref[0])
    o_ref[0] = jnp.dot(h.astype(BF16), w_ref[...], preferred_element_type=F32)


def _inproj(x, g, sh, sc, w_bf16, tm=512):
    b, l, d = x.shape
    n = w_bf16.shape[1]
    tm = min(tm, l)
    return pl.pallas_call(
        _inproj_kernel,
        grid=(b, l // tm),
        in_specs=[
            pl.BlockSpec((1, tm, d), lambda i, j: (i, j, 0)),
            pl.BlockSpec((1, d), lambda i, j: (0, 0)),
            pl.BlockSpec((1, 1, d), lambda i, j: (i, 0, 0)),
            pl.BlockSpec((1, 1, d), lambda i, j: (i, 0, 0)),
            _resident((d, n), lambda i, j: (0, 0)),
        ],
        out_specs=pl.BlockSpec((1, tm, n), lambda i, j: (i, j, 0)),
        out_shape=jax.ShapeDtypeStruct((b, l, n), F32),
        compiler_params=_params("arbitrary", "arbitrary"),
        name="inproj",
    )(x, g.reshape(1, d), sh.reshape(b, 1, d), sc.reshape(b, 1, d), w_bf16)


def _split3(x):
    hi = x.astype(BF16)
    r = x - hi.astype(F32)
    mid = r.astype(BF16)
    lo = (r - mid.astype(F32)).astype(BF16)
    return hi, mid, lo


def _cumgate_kernel(f_ref, b_ref, o_ref, *, cw):
    x = f_ref[0] + b_ref[...]
    lf = jnp.minimum(x, 0.0) - jnp.log1p(jnp.exp(-jnp.abs(x)))
    h, l = lf.shape
    row = lax.broadcasted_iota(I32, (cw, cw), 0)
    col = lax.broadcasted_iota(I32, (cw, cw), 1)
    tri = jnp.where(row <= col, 1.0, 0.0).astype(BF16)
    carry = jnp.zeros((h, 1), F32)
    for c in range(l // cw):
        hi, mid, lo = _split3(lf[:, c * cw:(c + 1) * cw])
        cs = (jnp.dot(hi, tri, preferred_element_type=F32)
              + jnp.dot(mid, tri, preferred_element_type=F32)
              + jnp.dot(lo, tri, preferred_element_type=F32)) + carry
        o_ref[0, :, c * cw:(c + 1) * cw] = cs * LOG2E
        carry = cs[:, cw - 1:cw]


def _cumgate(f_t, b_f):
    b, h, l = f_t.shape
    cw = min(256, l)
    return pl.pallas_call(
        functools.partial(_cumgate_kernel, cw=cw),
        grid=(b,),
        in_specs=[
            pl.BlockSpec((1, h, l), lambda i: (i, 0, 0)),
            pl.BlockSpec((h, 1), lambda i: (0, 0)),
        ],
        out_specs=pl.BlockSpec((1, h, l), lambda i: (i, 0, 0)),
        out_shape=jax.ShapeDtypeStruct((b, h, l), F32),
        compiler_params=_params("arbitrary"),
        name="cumgate",
    )(f_t, b_f.reshape(h, 1))


def _norm_rows(xh, g, extra_scale=1.0):
    ms = jnp.mean(xh * xh, axis=0, keepdims=True)
    return xh * lax.rsqrt(ms + EPS) * (g * extra_scale)


def _kaug_prep_kernel(*refs, n_heads, mode, tk):
    if mode == "cum":
        x_ref, g_ref, c_ref, o_ref = refs
    else:
        x_ref, g_ref, o_ref = refs
    j = pl.program_id(1)
    g = g_ref[...]
    r8 = lax.broadcasted_iota(I32, (AUG_ROWS, tk), 0)
    pad = jnp.zeros((LANES - HEAD_DIM - AUG_ROWS, tk), F32)
    if mode == "pos":
        kpos = lax.broadcasted_iota(I32, (AUG_ROWS, tk), 1) + j * tk
        a = (kpos // POS_SPLIT).astype(F32)
        bb = (kpos % POS_SPLIT).astype(F32)
        aug_pos = jnp.where(r8 < 3, a, jnp.where(r8 < 6, bb, 0.0))
    for hp in range((n_heads + 1) // 2):
        xt = x_ref[0, :, hp * LANES:(hp + 1) * LANES].T
        for hh in range(min(2, n_heads - 2 * hp)):
            h = 2 * hp + hh
            kn = _norm_rows(xt[hh * HEAD_DIM:(hh + 1) * HEAD_DIM], g)
            if mode == "cum":
                hi, mid, lo = _split3(c_ref[0, h, 0])
                aug = jnp.where(r8 == 0, hi.astype(F32),
                                jnp.where(r8 == 1, mid.astype(F32), jnp.where(r8 == 2, lo.astype(F32), 0.0)))
            elif mode == "pos":
                aug = aug_pos
            else:
                aug = jnp.zeros((AUG_ROWS, tk), F32)
            t = jnp.concatenate([kn, aug, pad], axis=0)
            o_ref[0, h, 0] = t.T.astype(o_ref.dtype)


def _kaug_prep(proj, g, col0, n_heads, tk, mode, cum=None):
    b, l, _ = proj.shape
    w = max(LANES, n_heads * HEAD_DIM)
    assert col0 % w == 0
    nk = l // tk
    in_specs = [
        pl.BlockSpec((1, tk, w), lambda i, j: (i, j, col0 // w)),
        pl.BlockSpec((HEAD_DIM, 1), lambda i, j: (0, 0)),
    ]
    args = [proj, g.reshape(HEAD_DIM, 1)]
    if mode == "cum":
        in_specs.append(pl.BlockSpec((1, n_heads, 1, 1, tk), lambda i, j: (i, 0, j, 0, 0)))
        args.append(cum)
    return pl.pallas_call(
        functools.partial(_kaug_prep_kernel, n_heads=n_heads, mode=mode, tk=tk),
        grid=(b, nk),
        in_specs=in_specs,
        out_specs=pl.BlockSpec((1, n_heads, 1, tk, LANES), lambda i, j: (i, 0, j, 0, 0)),
        out_shape=jax.ShapeDtypeStruct((b, n_heads, nk, tk, LANES), BF16),
        compiler_params=_params("arbitrary", "arbitrary"),
        name="kaug_prep",
    )(*args)


def _vat_prep_kernel(x_ref, o_ref, *, n_heads, tk):
    r16 = lax.broadcasted_iota(I32, (VA_ROWS - HEAD_DIM, tk), 0)
    ones_rows = jnp.where(r16 == 0, 1.0, 0.0).astype(o_ref.dtype)
    for hp in range(n_heads // 2):
        xt = x_ref[0, :, hp * LANES:(hp + 1) * LANES].T
        for hh in range(2):
            h = 2 * hp + hh
            o_ref[0, h, 0, :HEAD_DIM] = xt[hh * HEAD_DIM:(hh + 1) * HEAD_DIM].astype(o_ref.dtype)
            o_ref[0, h, 0, HEAD_DIM:] = ones_rows


def _vat_prep(proj, col0, n_heads, tk):
    b, l, _ = proj.shape
    w = n_heads * HEAD_DIM
    assert col0 % w == 0 and n_heads % 2 == 0
    nk = l // tk
    return pl.pallas_call(
        functools.partial(_vat_prep_kernel, n_heads=n_heads, tk=tk),
        grid=(b, nk),
        in_specs=[pl.BlockSpec((1, tk, w), lambda i, j: (i, j, col0 // w))],
        out_specs=pl.BlockSpec((1, n_heads, 1, VA_ROWS, tk), lambda i, j: (i, 0, j, 0, 0)),
        out_shape=jax.ShapeDtypeStruct((b, n_heads, nk, VA_ROWS, tk), BF16),
        compiler_params=_params("arbitrary", "arbitrary"),
        name="vat_prep",
    )(proj)


def _col_reduce(x, op):
    rows, n = x.shape
    ways = 8 if rows % (8 * SUBLANES) == 0 else 1
    x = op(x.reshape(ways, rows // (ways * SUBLANES), SUBLANES, n), axis=1)
    return op(op(x, axis=0), axis=0, keepdims=True)


def _softmax_step_t(s, m_ref, acc_ref, vat, idx):
    m_prev = m_ref[idx]
    m_new = jnp.maximum(m_prev, _col_reduce(s, jnp.max))
    p = jnp.exp2(s - m_new).astype(BF16)
    alpha = jnp.exp2(m_prev - m_new)
    acc_ref[idx] = alpha * acc_ref[idx] + jnp.dot(vat, p, preferred_element_type=F32)
    m_ref[idx] = m_new


def _pipelined_chunks(nch, qk, process):
    pairs = (nch - 1) // 2
    rest = nch - 2 * pairs
    qk(0, 0)

    def body(jj, carry):
        j = 2 * jj
        qk(j + 1, 1)
        process(j, 0, False)
        qk(j + 2, 0)
        process(j + 1, 1, False)
        return carry

    lax.fori_loop(0, pairs, body, 0)

    @pl.when(rest == 1)
    def _():
        process(2 * pairs, 0, True)

    @pl.when(rest == 2)
    def _():
        qk(2 * pairs + 1, 1)
        process(2 * pairs, 0, False)
        process(2 * pairs + 1, 1, True)


def _query_weights(q_t, g, scale, aug):
    qn = _norm_rows(q_t, g, scale * LOG2E)
    pad = jnp.zeros((LANES - HEAD_DIM - AUG_ROWS, q_t.shape[1]), F32)
    return jnp.concatenate([qn, aug, pad], axis=0).astype(BF16)


def _fox_kernel(q_ref, ka_ref, vat_ref, gq_ref, o_ref, s_ref, m_ref, acc_ref, *, tq, tk, hb, scale):
    qi = pl.program_id(2)
    gq = gq_ref[...]
    r8 = lax.broadcasted_iota(I32, (AUG_ROWS, tq), 0)
    aug = jnp.where(r8 < 3, -1.0, 0.0)
    q_t = q_ref[0].T
    ws = [_query_weights(q_t[h * HEAD_DIM:(h + 1) * HEAD_DIM], gq, scale, aug) for h in range(hb)]
    m_ref[...] = jnp.full(m_ref.shape, -jnp.inf, F32)
    acc_ref[...] = jnp.zeros(acc_ref.shape, F32)
    nch = ((qi + 1) * tq + tk - 1) // tk

    def qk(j, slot):
        for h in range(hb):
            s_ref[slot, h] = jnp.dot(ka_ref[0, h, j], ws[h], preferred_element_type=F32)

    def process(j, slot, last):
        for h in range(hb):
            s = s_ref[slot, h]
            if last:
                k_pos = lax.broadcasted_iota(I32, (tk, tq), 0) + j * tk
                q_pos = lax.broadcasted_iota(I32, (tk, tq), 1) + qi * tq
                s = jnp.where(k_pos <= q_pos, s, -jnp.inf)
            _softmax_step_t(s, m_ref, acc_ref, vat_ref[0, h, j], h)

    _pipelined_chunks(nch, qk, process)
    outs = []
    for h in range(hb):
        acc = acc_ref[h]
        outs.append(acc[:HEAD_DIM] / acc[HEAD_DIM:HEAD_DIM + 1])
    o_ref[0] = jnp.concatenate(outs, axis=0).T.astype(o_ref.dtype)


def _fox_attention(proj, ka, vat, g_q, tq, tk, hb=8):
    b, l, _ = proj.shape
    h, nk = ka.shape[1], ka.shape[2]
    dh = HEAD_DIM
    hb = math.gcd(hb, h)
    return pl.pallas_call(
        functools.partial(_fox_kernel, tq=tq, tk=tk, hb=hb, scale=dh ** -0.5),
        grid=(b, h // hb, l // tq),
        in_specs=[
            pl.BlockSpec((1, tq, hb * dh), lambda i, j, k: (i, k, j)),
            _resident((1, hb, nk, tk, LANES), lambda i, j, k: (i, j, 0, 0, 0)),
            _resident((1, hb, nk, VA_ROWS, tk), lambda i, j, k: (i, j, 0, 0, 0)),
            pl.BlockSpec((dh, 1), lambda i, j, k: (0, 0)),
        ],
        out_specs=pl.BlockSpec((1, tq, hb * dh), lambda i, j, k: (i, k, j)),
        out_shape=jax.ShapeDtypeStruct((b, l, h * dh), BF16),
        scratch_shapes=[
            pltpu.VMEM((2, hb, tk, tq), F32),
            pltpu.VMEM((hb, 1, tq), F32),
            pltpu.VMEM((hb, VA_ROWS, tq), F32),
        ],
        compiler_params=_params("arbitrary", "arbitrary", "arbitrary"),
        name="fox_attn",
    )(proj, ka, vat, g_q.reshape(dh, 1))


def _ffn_kernel(x_ref, o_ref, wo_ref, gt1_ref, g_ref, sh_ref, sc_ref, gt2_ref,
                wa_ref, wg_ref, cw_ref, cb_ref, wd_ref, out_ref, abuf_ref, *, tm, fc):
    i = pl.program_id(1)
    y = jnp.dot(o_ref[0], wo_ref[...], preferred_element_type=F32)
    x1 = x_ref[0] + gt1_ref[0] * y
    h = _modulated(x1, g_ref[...], sh_ref[0], sc_ref[0]).astype(BF16)

    @pl.when(i == 0)
    def _():
        abuf_ref[0:8, :] = jnp.zeros((8, abuf_ref.shape[1]), F32)

    @pl.when(i > 0)
    def _():
        abuf_ref[0:8, :] = abuf_ref[tm:tm + 8, :]

    f = abuf_ref.shape[1]
    y2 = None
    for c in range(f // fc):
        sl = slice(c * fc, (c + 1) * fc)
        a = jnp.dot(h, wa_ref[:, sl], preferred_element_type=F32)
        gate = jnp.dot(h, wg_ref[:, sl], preferred_element_type=F32)
        abuf_ref[8:8 + tm, sl] = a
        a1 = abuf_ref[7:7 + tm, sl]
        a2 = abuf_ref[6:6 + tm, sl]
        ac = cw_ref[0:1, sl] * a2 + cw_ref[1:2, sl] * a1 + cw_ref[2:3, sl] * a + cb_ref[:, sl]
        act = (ac * jax.nn.sigmoid(ac) * gate).astype(BF16)
        part = jnp.dot(act, wd_ref[sl, :], preferred_element_type=F32)
        y2 = part if y2 is None else y2 + part
    out_ref[0] = x1 + gt2_ref[0] * y2


def _post_attention_ffn(x, o, wo, gt1, g, sh, sc, gt2, wa, wg, cw, cb, wd, tm=512):
    b, l, d = x.shape
    f = wa.shape[1]
    tm = min(tm, l)
    row = lambda i, j: (i, j, 0)
    per_batch = lambda i, j: (i, 0, 0)
    const = lambda i, j: (0, 0)
    return pl.pallas_call(
        functools.partial(_ffn_kernel, tm=tm, fc=math.gcd(f, 256)),
        grid=(b, l // tm),
        in_specs=[
            pl.BlockSpec((1, tm, d), row),
            pl.BlockSpec((1, tm, d), row),
            _resident((d, d), const),
            pl.BlockSpec((1, 1, d), per_batch),
            pl.BlockSpec((1, d), const),
            pl.BlockSpec((1, 1, d), per_batch),
            pl.BlockSpec((1, 1, d), per_batch),
            pl.BlockSpec((1, 1, d), per_batch),
            _resident((d, f), const),
            _resident((d, f), const),
            pl.BlockSpec((3, f), const),
            pl.BlockSpec((1, f), const),
            _resident((f, d), const),
        ],
        out_specs=pl.BlockSpec((1, tm, d), row),
        out_shape=jax.ShapeDtypeStruct((b, l, d), F32),
        scratch_shapes=[pltpu.VMEM((tm + 8, f), F32)],
        compiler_params=_params("arbitrary", "arbitrary"),
        name="post_attn_ffn",
    )(x, o, wo, gt1.reshape(b, 1, d), g.reshape(1, d), sh.reshape(b, 1, d), sc.reshape(b, 1, d),
      gt2.reshape(b, 1, d), wa, wg, cw, cb.reshape(1, f), wd)


def _key_to_float(key):
    bits = jnp.where(key < 0, jnp.int32(-2 ** 31) - key, key)
    return lax.bitcast_convert_type(bits, F32)


def _float_to_key(x):
    bits = lax.bitcast_convert_type(x, I32)
    return jnp.where(bits < 0, jnp.int32(-2 ** 31) - bits, bits)


def _bf16_pieces(x):
    out, r = [], float(x)
    for _ in range(3):
        p = float(np.asarray(r, dtype=ml_dtypes.bfloat16).astype(np.float32))
        out.append(p)
        r -= p
    return tuple(out)


def _dsa_kernel(q_ref, qi_ref, wi_ref, ka_ref, vat_ref, ki_ref, gq_ref, o_ref,
                s_ref, b_ref, p_ref, m_ref, acc_ref, *, tq, tk, top_k, scale, slope_pieces):
    qt = pl.program_id(1)
    n_groups = ka_ref.shape[1]
    group = q_ref.shape[2] // HEAD_DIM // n_groups
    nch = ((qt + 1) * tq + tk - 1) // tk
    k_off = lax.broadcasted_iota(I32, (tk, tq), 0)
    q_pos = lax.broadcasted_iota(I32, (tk, tq), 1) + qt * tq
    kf = float(top_k)

    qi_t = qi_ref[0].T
    zpad = jnp.zeros((LANES - IDX_DIM, tq), F32)
    wqs = [jnp.concatenate([qi_t[hh * IDX_DIM:(hh + 1) * IDX_DIM], zpad], axis=0).astype(BF16)
           for hh in range(IDX_HEADS)]
    w_t = wi_ref[0].T[IDX_DIM:IDX_DIM + IDX_HEADS] * (IDX_HEADS ** -0.5 * IDX_DIM ** -0.5)

    def score_chunk(j, carry):
        kc = ki_ref[0, 0, j]
        sc = jnp.zeros((tk, tq), F32)
        for hh in range(IDX_HEADS):
            rel = jnp.maximum(jnp.dot(kc, wqs[hh], preferred_element_type=F32), 0.0)
            sc = sc + rel * w_t[hh:hh + 1]
        sc = jnp.where(k_off + j * tk <= q_pos, sc, -jnp.inf)
        s_ref[j] = sc
        b_ref[j] = sc.astype(BF16)
        return carry

    lax.fori_loop(0, nch, score_chunk, 0)

    def count(pred):
        ways = 2 if tk % (8 * SUBLANES) == 0 else 1

        def body(j, c):
            x = jnp.where(pred(s_ref[j], j), 1.0, 0.0).reshape(ways, tk // (ways * SUBLANES), SUBLANES, tq)
            for i in range(x.shape[1]):
                c = c + x[:, i]
            return c
        c = lax.fori_loop(0, nch, body, jnp.zeros((ways, SUBLANES, tq), F32))
        return jnp.sum(jnp.sum(c, axis=0), axis=0, keepdims=True)

    def count_coarse(c16):
        rows = 2 * SUBLANES
        ways = 2 if tk % (8 * rows) == 0 else 1
        one, zero = jnp.ones((), BF16), jnp.zeros((), BF16)

        def body(j, c):
            x = jnp.where(b_ref[j] >= c16, one, zero).reshape(ways, tk // (ways * rows), rows, tq)
            for i in range(x.shape[1]):
                c = c + x[:, i]
            return c
        c = lax.fori_loop(0, nch, body, jnp.zeros((ways, rows, tq), BF16))
        return jnp.sum(jnp.sum(c.astype(F32), axis=0), axis=0, keepdims=True)

    def search(_):
        def coarse_step(_, lohi):
            lo, hi = lohi
            mid = (lo + hi) >> 1
            ge = count_coarse(_key_to_float(mid << 16).astype(BF16)) >= kf
            return jnp.where(ge, mid, lo), jnp.where(ge, hi, mid)

        lo16, hi16 = lax.fori_loop(0, 16, coarse_step, (jnp.full((1, tq), KEY_NEG_INF >> 16, I32),
                                                        jnp.full((1, tq), (KEY_POS_INF >> 16) + 1, I32)))

        stand_in = 2.0 ** 24

        def unsettled(lo, hi, clo):
            return (clo != kf) & (hi > lo + 1)

        def any_of(mask):
            return jnp.max(jnp.where(mask, 1.0, 0.0))

        def extremes(f_lo, f_hi):
            ways = 2 if tk % (8 * SUBLANES) == 0 else 1
            shape4 = (ways, tk // (ways * SUBLANES), SUBLANES, tq)

            def body(j, c):
                vmax, vmin = c
                s = s_ref[j]
                below = jnp.where(s < f_hi, s, -jnp.inf).reshape(shape4)
                above = jnp.where(s >= f_lo, s, jnp.inf).reshape(shape4)
                for i in range(shape4[1]):
                    vmax = jnp.maximum(vmax, below[:, i])
                    vmin = jnp.minimum(vmin, above[:, i])
                return vmax, vmin

            vmax, vmin = lax.fori_loop(0, nch, body, (jnp.full((ways, SUBLANES, tq), -jnp.inf, F32),
                                                      jnp.full((ways, SUBLANES, tq), jnp.inf, F32)))
            return (jnp.max(jnp.max(vmax, axis=0), axis=0, keepdims=True),
                    jnp.min(jnp.min(vmin, axis=0), axis=0, keepdims=True))

        def one_round(rs):
            lo, hi, clo, chi, done, tau_s, nge_s, _ = rs

            def live(lo, hi, clo, chi, stall):
                return unsettled(lo, hi, clo) & (done == 0.0) & (clo - chi > 2.0) & (stall < 3.0)

            def cond(st):
                return st[5] > 0.0

            def step(st):
                lo, hi, clo, chi, stall, _ = st
                active = live(lo, hi, clo, chi, stall)
                mid = (lo >> 1) + (hi >> 1) + (lo & hi & 1)
                cmid = _key_to_float(mid)
                cnt = count(lambda s, j: s >= cmid)
                ge = cnt >= kf
                split_nothing = (cnt == clo) | (cnt == chi)
                stall = jnp.where(active, jnp.where(split_nothing, stall + 1.0, 0.0), stall)
                take_lo = active & ge
                take_hi = active & jnp.logical_not(ge)
                lo = jnp.where(take_lo, mid, lo)
                clo = jnp.where(take_lo, cnt, clo)
                hi = jnp.where(take_hi, mid, hi)
                chi = jnp.where(take_hi, cnt, chi)
                return lo, hi, clo, chi, stall, any_of(live(lo, hi, clo, chi, stall))

            stall0 = jnp.zeros((1, tq), F32)
            lo, hi, clo, chi, _, _ = lax.while_loop(
                cond, step, (lo, hi, clo, chi, stall0, any_of(live(lo, hi, clo, chi, stall0))))

            pending = unsettled(lo, hi, clo) & (done == 0.0)
            zero = jnp.zeros((1, tq), F32)
            vmax, vmin = lax.cond(any_of(pending) > 0.0,
                                  lambda _: extremes(_key_to_float(lo), _key_to_float(hi)),
                                  lambda _: (zero, zero), 0)
            settle = pending & ((clo - chi <= 2.0) | ((vmax == vmin) & (clo < stand_in)))
            tau_s = jnp.where(settle, vmax, tau_s)
            nge_s = jnp.where(settle, jnp.where(vmax != vmin, kf, clo), nge_s)
            done = jnp.where(settle, 1.0, done)
            tighten = pending & jnp.logical_not(settle) & (vmax >= vmin)
            lo = jnp.where(tighten, _float_to_key(vmin), lo)
            hi = jnp.where(tighten, _float_to_key(vmax) + 1, hi)
            return lo, hi, clo, chi, done, tau_s, nge_s, any_of(unsettled(lo, hi, clo) & (done == 0.0))

        lo0 = jnp.maximum((lo16 - 1) << 16, KEY_NEG_INF)
        hi0 = hi16 << 16
        zeros = jnp.zeros((1, tq), F32)
        rs0 = (lo0, hi0, jnp.full((1, tq), stand_in, F32), zeros, zeros, zeros, zeros, jnp.float32(1.0))
        lo, hi, clo, chi, done, tau_s, nge_s, _ = lax.while_loop(lambda rs: rs[7] > 0.0, one_round, rs0)
        f_lo = _key_to_float(lo)
        never_moved = lo == lo0
        clo = lax.cond(any_of(never_moved) > 0.0,
                       lambda _: jnp.where(never_moved, count(lambda s, j: s >= f_lo), clo),
                       lambda _: clo, 0)
        return jnp.where(done > 0.0, tau_s, f_lo), jnp.where(done > 0.0, nge_s, clo)

    def no_search(_):
        return jnp.full((1, tq), -jnp.inf, F32), jnp.full((1, tq), kf, F32)

    tau, n_ge = lax.cond((qt + 1) * tq > top_k, search, no_search, 0)
    tau_sel = jnp.maximum(tau, F32_LOWEST)

    tie = (n_ge > kf) & (tau > -jnp.inf)
    any_tie = jnp.max(jnp.where(tie, 1.0, 0.0)) > 0.0

    def mask_plain(_):
        def body(j, carry):
            s_ref[j] = jnp.where(s_ref[j] >= tau_sel, 0.0, NEG_BIG)
            return carry
        lax.fori_loop(0, nch, body, 0)
        return 0

    def mask_ties(_):
        need = kf - count(lambda s, j: s > tau)

        def step(_, lohi):
            lo, hi = lohi
            mid = (lo + hi) >> 1
            ge = count(lambda s, j: (s == tau) & (k_off + j * tk <= mid)) >= need
            return jnp.where(ge, lo, mid), jnp.where(ge, mid, hi)

        steps = max(1, int(math.ceil(math.log2(s_ref.shape[0] * tk + 1))))
        lo0 = jnp.full((1, tq), -1, I32)
        hi0 = jnp.full((1, tq), 1, I32) * (nch * tk - 1)
        _, hi = lax.fori_loop(0, steps, step, (lo0, hi0))
        jstar = jnp.where(tie, hi, jnp.int32(2 ** 30))

        def body(j, carry):
            s = s_ref[j]
            eq_bias = jnp.where(k_off + j * tk <= jstar, 0.0, NEG_BIG)
            s_ref[j] = jnp.where(s > tau_sel, 0.0, jnp.where(s == tau_sel, eq_bias, NEG_BIG))
            return carry
        lax.fori_loop(0, nch, body, 0)
        return 0

    lax.cond(any_tie, mask_ties, mask_plain, 0)

    gq = gq_ref[...]
    q_t = q_ref[0].T
    r8 = lax.broadcasted_iota(I32, (AUG_ROWS, tq), 0)
    hb = m_ref.shape[0]
    for h0 in range(0, n_groups * group, hb):
        heads = list(range(h0, h0 + hb))
        ws = []
        for hq in heads:
            hi, mid, lo = slope_pieces[hq]
            rows = (POS_SPLIT * hi, POS_SPLIT * mid, POS_SPLIT * lo, hi, mid, lo)
            aug = jnp.zeros((AUG_ROWS, tq), F32)
            for i, val in enumerate(rows):
                aug = jnp.where(r8 == i, val, aug)
            ws.append(_query_weights(q_t[hq * HEAD_DIM:(hq + 1) * HEAD_DIM], gq, scale, aug))
        m_ref[...] = jnp.full(m_ref.shape, -jnp.inf, F32)
        acc_ref[...] = jnp.zeros(acc_ref.shape, F32)

        def qk(j, slot, heads=heads, ws=ws):
            for i, hq in enumerate(heads):
                p_ref[slot, i] = jnp.dot(ka_ref[0, hq // group, j], ws[i], preferred_element_type=F32) + s_ref[j]

        def process(j, slot, last, heads=heads):
            for i, hq in enumerate(heads):
                _softmax_step_t(p_ref[slot, i], m_ref, acc_ref, vat_ref[0, hq // group, j], i)

        _pipelined_chunks(nch, qk, process)
        outs = []
        for i in range(hb):
            acc = acc_ref[i]
            outs.append(acc[:HEAD_DIM] / acc[HEAD_DIM:HEAD_DIM + 1])
        o_ref[0, :, h0 * HEAD_DIM:(h0 + hb) * HEAD_DIM] = jnp.concatenate(outs, axis=0).T.astype(o_ref.dtype)


def _dsa_attention(proj, ka, vat, ki, g_q, n_q_heads, qi_col0, wi_col0, tq, tk, top_k):
    b, l, _ = proj.shape
    dh = HEAD_DIM
    n_groups, nk = ka.shape[1], ka.shape[2]
    qw = n_q_heads * dh
    iw = IDX_HEADS * IDX_DIM
    assert qi_col0 % iw == 0 and wi_col0 % LANES == IDX_DIM
    slope_pieces = tuple(_bf16_pieces(2.0 ** (-8.0 * (i + 1) / n_q_heads) * LOG2E) for i in range(n_q_heads))
    hb = math.gcd(16, n_q_heads)
    return pl.pallas_call(
        functools.partial(_dsa_kernel, tq=tq, tk=tk, top_k=top_k, scale=dh ** -0.5, slope_pieces=slope_pieces),
        grid=(b, l // tq),
        in_specs=[
            pl.BlockSpec((1, tq, qw), lambda i, j: (i, j, 0)),
            pl.BlockSpec((1, tq, iw), lambda i, j: (i, j, qi_col0 // iw)),
            pl.BlockSpec((1, tq, LANES), lambda i, j: (i, j, wi_col0 // LANES)),
            _resident((1, n_groups, nk, tk, LANES), lambda i, j: (i, 0, 0, 0, 0)),
            _resident((1, n_groups, nk, VA_ROWS, tk), lambda i, j: (i, 0, 0, 0, 0)),
            _resident((1, 1, nk, tk, LANES), lambda i, j: (i, 0, 0, 0, 0)),
            pl.BlockSpec((dh, 1), lambda i, j: (0, 0)),
        ],
        out_specs=pl.BlockSpec((1, tq, qw), lambda i, j: (i, j, 0)),
        out_shape=jax.ShapeDtypeStruct((b, l, qw), BF16),
        scratch_shapes=[
            pltpu.VMEM((nk, tk, tq), F32),
            pltpu.VMEM((nk, tk, tq), BF16),
            pltpu.VMEM((2, hb, tk, tq), F32),
            pltpu.VMEM((hb, 1, tq), F32),
            pltpu.VMEM((hb, VA_ROWS, tq), F32),
        ],
        compiler_params=_params("arbitrary", "arbitrary"),
        name="dsa_attn",
    )(proj, proj, proj, ka, vat, ki, g_q.reshape(dh, 1))


def _pad_cols(w, mult=LANES):
    n = w.shape[1]
    return jnp.pad(w, ((0, 0), (0, _round_up(n, mult) - n)))


def kernel(x, c, w_ada, b_ada, g_norm_mix, g_norm_ffn, fox_w_in, fox_b_f, fox_g_q, fox_g_k, fox_w_out,
           dsa_w_in, dsa_g_q, dsa_g_k, dsa_g_kidx, dsa_w_out, ffn_w_up, ffn_conv_w, ffn_conv_b, ffn_w_down):
    b, l, d = x.shape
    dh = HEAD_DIM
    n_heads = d // dh
    d_ff = ffn_w_down.shape[1]
    top_k = min(TOPK_MAX, l // 4)
    tq = min(Q_TILE, l)
    tk = min(K_TILE, l)

    mod = _adaln(c, w_ada, b_ada)

    def ffn(i, xx, o, w_out):
        sh2, sc2, gt2 = mod[i, :, 3 * d:4 * d], mod[i, :, 4 * d:5 * d], mod[i, :, 5 * d:6 * d]
        gt1 = mod[i, :, 2 * d:3 * d]
        wup = ffn_w_up[i].astype(BF16)
        return _post_attention_ffn(xx, o, w_out.astype(BF16), gt1, g_norm_ffn[i], sh2, sc2, gt2,
                                   wup[:, :d_ff], wup[:, d_ff:], ffn_conv_w[i], ffn_conv_b[i],
                                   ffn_w_down[i].astype(BF16))

    proj = _inproj(x, g_norm_mix[0], mod[0, :, 0:d], mod[0, :, d:2 * d], _pad_cols(fox_w_in[0]).astype(BF16))
    f_t = proj[..., 3 * d:3 * d + n_heads].transpose(0, 2, 1)
    cum = _cumgate(f_t, fox_b_f[0]).reshape(b, n_heads, l // tk, 1, tk)
    ka = _kaug_prep(proj, fox_g_k[0], d, n_heads, tk, "cum", cum)
    vat = _vat_prep(proj, 2 * d, n_heads, tk)
    o = _fox_attention(proj, ka, vat, fox_g_q[0], min(FOX_Q_TILE, l), tk)
    x = ffn(0, x, o, fox_w_out[0])

    g_kv = DSA_KV_HEADS
    kvw = g_kv * dh
    proj = _inproj(x, g_norm_mix[1], mod[1, :, 0:d], mod[1, :, d:2 * d], _pad_cols(dsa_w_in[0]).astype(BF16))
    o0, o1, o2 = d, d + kvw, d + 2 * kvw
    o3 = o2 + IDX_HEADS * IDX_DIM
    o4 = o3 + IDX_DIM
    ka = _kaug_prep(proj, dsa_g_k[0], o0, g_kv, tk, "pos")
    vat = _vat_prep(proj, o1, g_kv, tk)
    ki = _kaug_prep(proj, dsa_g_kidx[0], o3, 1, tk, "none")
    o = _dsa_attention(proj, ka, vat, ki, dsa_g_q[0], n_heads, o2, o4, tq, tk, top_k)
    x = ffn(1, x, o, dsa_w_out[0])
    return x
```
